```python
import jax, jax.numpy as jnp
from jax import lax
import numpy as np

D_MODEL = 1024
BATCH = 32
SEQ = 256
DEPTH = 1
DEC_BATCH = 8
DEC_SEQ = 1024
PAST_LEN = 256

GRID_W = 64
D_MIX = D_MODEL
HG_HEADS = 4
HG_DK = D_MIX // 8
HG_DV = D_MIX // 8
HG_W = HG_HEADS * HG_DK
RG_W = D_MIX - HG_W
RG_HEADS = 8
RG_BW = RG_W // RG_HEADS
RG_C = 8.0
CONV_W = 4
CONV_PAD = (CONV_W // 2, CONV_W - 1 - CONV_W // 2)
CHUNK = 32
D_IN = 5 * HG_W + 2 * RG_W
SPLITS = [HG_W, 2 * HG_W, 3 * HG_W, 4 * HG_W, 5 * HG_W, 5 * HG_W + RG_W]
PEER_HEADS = 8
PEER_NKEYS = 128
PEER_EXPERTS = PEER_NKEYS * PEER_NKEYS
PEER_TOPK = 16
PEER_DQ = 256
PEER_BLOCK = 128
ALPHA = (2.0 * DEPTH) ** 0.25
BETA = (8.0 * DEPTH) ** -0.25
LN_EPS = 1e-5
RMS_EPS = 1e-6

kernel_name = 'hymba_hgrn2_rglru_peer_diffusion_step'

F32 = jnp.float32


def _layer_norm(x, g, b):
    xf = x.astype(F32)
    mu = jnp.mean(xf, -1, keepdims=True)
    var = jnp.mean(jnp.square(xf - mu), -1, keepdims=True)
    return ((xf - mu) * lax.rsqrt(var + LN_EPS) * g.astype(F32) + b.astype(F32)).astype(x.dtype)


def _dwconv(x, w, b):
    y = lax.conv_general_dilated(x, w[:, None, :].astype(x.dtype), (1,), [CONV_PAD],
                                 dimension_numbers=('NWC', 'WIO', 'NWC'),
                                 feature_group_count=x.shape[-1])
    return y + b.astype(x.dtype)


def _hgrn2_chunk(q, k, v, log_f, s0):
    bsz, t, nh, _ = q.shape
    dv = v.shape[-1]
    nc = t // CHUNK

    def ch(a):
        return a.reshape(bsz, nc, CHUNK, nh, a.shape[-1])
    q, k, v, log_f = ch(q), ch(k), ch(v), ch(log_f)
    b = jnp.cumsum(log_f, axis=2)
    b_last = b[:, :, -1:]
    q_dec = q * jnp.exp(b)
    k_dec = k * jnp.exp(-b)
    k_end = k * jnp.exp(b_last - b)
    mask = jnp.tril(jnp.ones((CHUNK, CHUNK), dtype=bool))
    att = jnp.where(mask, jnp.einsum('bnchd,bnshd->bnhcs', q_dec, k_dec), 0.0)
    o_intra = jnp.einsum('bnhcs,bnshe->bnche', att, v)
    d_state = jnp.einsum('bnchd,bnche->nbhde', k_end, v)
    decay = jnp.moveaxis(jnp.exp(b_last[:, :, 0]), 1, 0)

    def step(s, inp):
        g_n, ds_n = inp
        return g_n[..., None] * s + ds_n, s
    s_final, s_prev = lax.scan(step, s0, (decay, d_state))
    o_inter = jnp.einsum('bnchd,nbhde->bnche', q_dec, s_prev)
    return (o_intra + o_inter).reshape(bsz, t, nh, dv), s_final


def _hgrn2_bidir(q, v, f_pre, g, lb, norm_g, s0):
    bsz, t, _ = q.shape

    def heads(a):
        return a.astype(F32).reshape(bsz, t, HG_HEADS, -1)
    qh = heads(jax.nn.silu(q))
    vh = heads(v)
    outs, states = [], []
    for d in range(2):
        z = f_pre[d].astype(F32)
        f = lb[d] + (1.0 - lb[d]) * jax.nn.sigmoid(z)
        k = (1.0 - lb[d]) * jax.nn.sigmoid(-z)
        args = (qh, heads(k), vh, jnp.log(heads(f)))
        if d == 1:
            args = tuple(jnp.flip(a, axis=1) for a in args)
        od, sd = _hgrn2_chunk(*args, s0[:, d].astype(F32))
        outs.append(od if d == 0 else jnp.flip(od, axis=1))
        states.append(sd)
    o = outs[0] + outs[1]
    o = o * lax.rsqrt(jnp.mean(o * o, -1, keepdims=True) + RMS_EPS) * norm_g.astype(F32).reshape(HG_HEADS, HG_DV)
    o = o.reshape(bsz, t, HG_W).astype(g.dtype) * jax.nn.silu(g)
    return o, jnp.stack(states, axis=1)


def _lin_comb(left, right):
    a1, u1 = left
    a2, u2 = right
    return a1 * a2, a2 * u1 + u2


def _rglru(x, w_r, b_r, w_i, b_i, lam, h0):
    bsz, t, _ = x.shape
    xb = x.reshape(bsz, t, RG_HEADS, RG_BW)

    def gate(w, b):
        return jax.nn.sigmoid(jnp.einsum('btnc,ncd->btnd', xb, w.astype(F32)).reshape(bsz, t, RG_W) + b.astype(F32))
    r = gate(w_r, b_r)
    i = gate(w_i, b_i)
    log_a = -RG_C * r * jax.nn.softplus(-lam.astype(F32))
    a = jnp.exp(log_a)
    u = jnp.sqrt(-jnp.expm1(2.0 * log_a)) * (i * x)
    a_cum, h = lax.associative_scan(_lin_comb, (a, u), axis=1)
    h = h + a_cum * h0[:, None, :]
    return h, h[:, -1]


def _rglru_bidir(x, lp, h0):
    x = x.astype(F32)
    hs, states = [], []
    for d in range(2):
        xd = x if d == 0 else jnp.flip(x, axis=1)
        h, hl = _rglru(xd, lp['rg_wr'][d], lp['rg_br'][d], lp['rg_wi'][d], lp['rg_bi'][d],
                       lp['rg_lam'][d], h0[:, d].astype(F32))
        hs.append(h if d == 0 else jnp.flip(h, axis=1))
        states.append(hl)
    return hs[0] + hs[1], jnp.stack(states, axis=1)


def _token_mixer(h, s_hg0, s_rg0, grid, lp):
    bsz, t, _ = h.shape
    z = h @ lp['w_in']
    q, iv, f_f, f_b, g, xr, gr = jnp.split(z, SPLITS, axis=-1)
    o_hg, s_hg = _hgrn2_bidir(q, iv, (f_f, f_b), g, lp['lb'], lp['hg_norm'], s_hg0)
    if grid:
        rows = t // GRID_W
        xr = xr.reshape(bsz, rows, GRID_W, RG_W).transpose(0, 2, 1, 3)
        xc = _dwconv(xr.reshape(bsz * GRID_W, rows, RG_W), lp['conv_w'], lp['conv_b'])
        xc = xc.reshape(bsz, GRID_W * rows, RG_W)
    else:
        xc = _dwconv(xr, lp['conv_w'], lp['conv_b'])
    hr, s_rg = _rglru_bidir(xc, lp, s_rg0)
    if grid:
        hr = hr.reshape(bsz, GRID_W, rows, RG_W).transpose(0, 2, 1, 3).reshape(bsz, t, RG_W)
    y_rg = hr.astype(h.dtype) * jax.nn.gelu(gr)
    out = jnp.concatenate([o_hg.astype(h.dtype), y_rg], axis=-1) @ lp['w_out']
    return out, s_hg, s_rg


def _peer(h, wq, keys, u_tab, v_tab):
    bsz, t, d = h.shape
    blocks = h.reshape(-1, PEER_BLOCK, d)

    def blk(xb):
        m = xb.shape[0]
        q = (xb @ wq).reshape(m, PEER_HEADS, 2, PEER_DQ // 2)
        s = jnp.einsum('mhpd,hpkd->mhpk', q, keys)
        sv, si = lax.top_k(s, PEER_TOPK)
        cand = (sv[:, :, 0, :, None] + sv[:, :, 1, None, :]).reshape(m, PEER_HEADS, PEER_TOPK * PEER_TOPK)
        cand_idx = (si[:, :, 0, :, None] * PEER_NKEYS + si[:, :, 1, None, :]).reshape(m, PEER_HEADS, PEER_TOPK * PEER_TOPK)
        fv, fi = lax.top_k(cand, PEER_TOPK)
        idx = jnp.take_along_axis(cand_idx, fi, axis=-1)
        gate = jax.nn.softmax(fv.astype(F32), axis=-1).astype(xb.dtype)
        act = jax.nn.gelu(jnp.einsum('md,mhkd->mhk', xb, u_tab[idx]))
        return jnp.einsum('mhk,mhkd->md', gate * act, v_tab[idx])
    return lax.map(blk, blocks).reshape(bsz, t, d)


def _layer(x, cond, s_hg0, s_rg0, grid, lp):
    mod = jax.nn.silu(cond) @ lp['w_ada'] + lp['b_ada']
    sh1, sc1, g1, sh2, sc2, g2 = jnp.split(mod[:, None, :], 6, axis=-1)
    h = x * (1.0 + sc1) + sh1
    mix, s_hg, s_rg = _token_mixer(h, s_hg0, s_rg0, grid, lp)
    x = _layer_norm(ALPHA * x + g1 * mix, lp['ln1_g'], lp['ln1_b'])
    h = x * (1.0 + sc2) + sh2
    ff = _peer(h, lp['peer_wq'], lp['peer_keys'], lp['peer_u'], lp['peer_v'])
    x = _layer_norm(ALPHA * x + g2 * ff, lp['ln2_g'], lp['ln2_b'])
    return x, s_hg, s_rg


def setup_inputs(seed: int = 0) -> dict:
    key = jax.random.key(seed)
    ks = iter(jax.random.split(key, 40))

    def nrm(shape, scale):
        return jax.random.normal(next(ks), shape, F32) * scale
    u = jax.random.uniform(next(ks), (DEPTH, 2, RG_W), F32, minval=0.9, maxval=0.999)
    s = u ** (1.0 / RG_C)
    rg_lam = jnp.log(s) - jnp.log1p(-s)
    return {
        'x_prompt': nrm((BATCH, SEQ, D_MODEL), 1.0),
        'x_sample': nrm((DEC_BATCH, DEC_SEQ, D_MODEL), 1.0),
        'c': nrm((DEC_BATCH, D_MODEL), 1.0),
        'state_hgrn': nrm((DEC_BATCH, DEPTH, 2, HG_HEADS, HG_DK, HG_DV), 0.3),
        'state_rglru': nrm((DEC_BATCH, DEPTH, 2, RG_W), 0.5),
        'c_ctx': nrm((D_MODEL,), 1.0),
        'w_ada': nrm((DEPTH, D_MODEL, 6 * D_MODEL), 0.3 * D_MODEL ** -0.5),
        'b_ada': nrm((DEPTH, 6 * D_MODEL), 0.02),
        'w_in': nrm((DEPTH, D_MODEL, D_IN), D_MODEL ** -0.5),
        'hgrn_lb': nrm((2, DEPTH + 1, HG_W), 0.1),
        'hgrn_norm_g': 1.0 + nrm((DEPTH, HG_W), 0.02),
        'conv_w': nrm((DEPTH, CONV_W, RG_W), CONV_W ** -0.5),
        'conv_b': nrm((DEPTH, RG_W), 0.02),
        'rg_wr': nrm((DEPTH, 2, RG_HEADS, RG_BW, RG_BW), RG_BW ** -0.5),
        'rg_br': nrm((DEPTH, 2, RG_W), 0.02),
        'rg_wi': nrm((DEPTH, 2, RG_HEADS, RG_BW, RG_BW), RG_BW ** -0.5),
        'rg_bi': nrm((DEPTH, 2, RG_W), 0.02),
        'rg_lam': rg_lam,
        'w_out': nrm((DEPTH, D_MIX, D_MODEL), BETA * D_MIX ** -0.5),
        'ln1_g': 1.0 + nrm((DEPTH, D_MODEL), 0.02),
        'ln1_b': nrm((DEPTH, D_MODEL), 0.02),
        'peer_wq': nrm((DEPTH, D_MODEL, PEER_HEADS * PEER_DQ), D_MODEL ** -0.5),
        'peer_keys': nrm((DEPTH, PEER_HEADS, 2, PEER_NKEYS, PEER_DQ // 2), (PEER_DQ // 2) ** -0.5),
        'peer_u': nrm((DEPTH, PEER_EXPERTS, D_MODEL), D_MODEL ** -0.5),
        'peer_v': nrm((DEPTH, PEER_EXPERTS, D_MODEL), BETA),
        'ln2_g': 1.0 + nrm((DEPTH, D_MODEL), 0.02),
        'ln2_b': nrm((DEPTH, D_MODEL), 0.02),
    }


def reference(x_prompt, x_sample, c, state_hgrn, state_rglru, c_ctx, w_ada, b_ada, w_in,
              hgrn_lb, hgrn_norm_g, conv_w, conv_b, rg_wr, rg_br, rg_wi, rg_bi, rg_lam,
              w_out, ln1_g, ln1_b, peer_wq, peer_keys, peer_u, peer_v, ln2_g, ln2_b):
    lb_all = jnp.cumsum(jax.nn.softmax(hgrn_lb.astype(F32), axis=1), axis=1)
    xp, xs = x_prompt, x_sample
    bp = x_prompt.shape[0]
    new_hg, new_rg = [], []
    for l in range(DEPTH):
        lp = {
            'w_ada': w_ada[l], 'b_ada': b_ada[l], 'w_in': w_in[l], 'lb': lb_all[:, l],
            'hg_norm': hgrn_norm_g[l], 'conv_w': conv_w[l], 'conv_b': conv_b[l],
            'rg_wr': rg_wr[l], 'rg_br': rg_br[l], 'rg_wi': rg_wi[l], 'rg_bi': rg_bi[l],
            'rg_lam': rg_lam[l], 'w_out': w_out[l], 'ln1_g': ln1_g[l], 'ln1_b': ln1_b[l],
            'peer_wq': peer_wq[l], 'peer_keys': peer_keys[l], 'peer_u': peer_u[l],
            'peer_v': peer_v[l], 'ln2_g': ln2_g[l], 'ln2_b': ln2_b[l],
        }
        zero_hg = jnp.zeros((bp, 2, HG_HEADS, HG_DK, HG_DV), F32)
        zero_rg = jnp.zeros((bp, 2, RG_W), F32)
        xp, s_hg, s_rg = _layer(xp, c_ctx[None, :], zero_hg, zero_rg, False, lp)
        new_hg.append(s_hg)
        new_rg.append(s_rg)
        xs, _, _ = _layer(xs, c, state_hgrn[:, l], state_rglru[:, l], True, lp)
    new_state_hgrn = jnp.stack(new_hg, axis=1).astype(x_prompt.dtype)
    new_state_rglru = jnp.stack(new_rg, axis=1).astype(x_prompt.dtype)
    return (xp, xs, new_state_hgrn, new_state_rglru)
```

```python
import functools

import jax
import jax.numpy as jnp
from jax import lax
from jax.experimental import pallas as pl
from jax.experimental.pallas import tpu as pltpu

F32 = jnp.float32
BF16 = jnp.bfloat16

D_MODEL = 1024
HG_HEADS = 4
HG_DK = 128
HG_W = 512
RG_W = 512
RG_HEADS = 8
RG_BW = 64
RG_C = 8.0
CONV_W = 4
CONV_LEFT = CONV_W // 2
GRID_W = 64
CHUNK = 32
D_IN = 5 * HG_W + 2 * RG_W
PEER_HEADS = 8
PEER_NKEYS = 128
PEER_TOPK = 16
PEER_DQ = 256
DEPTH = 1
ALPHA = (2.0 * DEPTH) ** 0.25
LN_EPS = 1e-5
RMS_EPS = 1e-6

LANES = 128
SUBLANES = 8
VMEM_CAP_BYTES = 56 * 1024 * 1024

TOK_TILE = 256
GROUP = 128
GPS = SUBLANES * CHUNK // GROUP
PEER_TM = 512
PEER_EB = 1024
ROUTE_TM = 256


def _cparams(n_axes, vmem_bytes):
    return pltpu.CompilerParams(
        dimension_semantics=("arbitrary",) * n_axes,
        vmem_limit_bytes=min(int(vmem_bytes), VMEM_CAP_BYTES),
    )


def _silu(x):
    return x * jax.nn.sigmoid(x)


def _gelu(x):
    return jax.nn.gelu(x, approximate=True)


def _dot(a, b):
    return jnp.dot(a, b, preferred_element_type=F32)


def _dot_nt(a, b):
    return lax.dot_general(a, b, (((1,), (1,)), ((), ())), preferred_element_type=F32)


def _layer_norm(x, g, b):
    mu = jnp.mean(x, -1, keepdims=True)
    xc = x - mu
    var = jnp.mean(xc * xc, -1, keepdims=True)
    return xc * lax.rsqrt(var + LN_EPS) * g + b


def _mod_kernel(cond_ref, w_ref, b_ref, o_ref):
    c = _silu(cond_ref[...]).astype(BF16)
    o_ref[...] = _dot(c, w_ref[...].astype(BF16)) + b_ref[...]


def _mod_call(cond, w_ada, b_ada):
    rows = cond.shape[0]
    nblk = w_ada.shape[1] // D_MODEL
    return pl.pallas_call(
        _mod_kernel,
        grid=(nblk,),
        in_specs=[
            pl.BlockSpec((rows, D_MODEL), lambda j: (0, 0)),
            pl.BlockSpec((D_MODEL, D_MODEL), lambda j: (0, j)),
            pl.BlockSpec((1, D_MODEL), lambda j: (0, j)),
        ],
        out_specs=pl.BlockSpec((rows, D_MODEL), lambda j: (0, j)),
        out_shape=jax.ShapeDtypeStruct((rows, w_ada.shape[1]), F32),
        compiler_params=_cparams(1, 24 << 20),
        name="mod",
    )(cond, w_ada, b_ada)


def _mask01(cond):
    return jnp.where(cond, 1.0, 0.0).astype(BF16)


def _split2(x):
    hi = x.astype(BF16)
    lo = (x - hi.astype(F32)).astype(BF16)
    return hi, lo


def _inproj_kernel(x_ref, mod_ref, w_ref, lb_ref,
                   qdf_ref, kdf_ref, kef_ref, qdb_ref, kdb_ref, keb_ref,
                   v_ref, vt_ref, dec_ref, sg_ref, xr_ref, ggr_ref):
    tt = x_ref.shape[0]
    mod = mod_ref[0]
    sh1 = mod[:, 0:D_MODEL]
    sc1 = mod[:, D_MODEL:2 * D_MODEL]
    h = (x_ref[...] * (1.0 + sc1) + sh1).astype(BF16)
    z = _dot(h, w_ref[...])
    q = z[:, 0:HG_W]
    iv = z[:, HG_W:2 * HG_W]
    g = z[:, 4 * HG_W:5 * HG_W]
    xr = z[:, 5 * HG_W:5 * HG_W + RG_W]
    gr = z[:, 5 * HG_W + RG_W:]
    qs = _silu(q)
    ivb = iv.astype(BF16)
    v_ref[...] = ivb
    ivt = iv.T.astype(BF16)
    for gi in range(tt // GROUP):
        vt_ref[gi] = ivt[:, gi * GROUP:(gi + 1) * GROUP]
    sg_ref[...] = _silu(g)
    xr_ref[...] = xr
    ggr_ref[...] = _gelu(gr)

    lbp = lb_ref[...]
    mx = jnp.max(lbp, axis=1, keepdims=True)
    e = jnp.exp(lbp - mx)
    lb_all = e[:, 0, :] / jnp.sum(e, axis=1)

    row = lax.broadcasted_iota(jnp.int32, (tt, tt), 0)
    col = lax.broadcasted_iota(jnp.int32, (tt, tt), 1)
    same = (row // CHUNK) == (col // CHUNK)
    tot = _mask01(same)
    srow = lax.broadcasted_iota(jnp.int32, (tt // CHUNK, tt), 0)
    scol = lax.broadcasted_iota(jnp.int32, (tt // CHUNK, tt), 1)
    sel = _mask01(srow == scol // CHUNK)

    outs = ((qdf_ref, kdf_ref, kef_ref), (qdb_ref, kdb_ref, keb_ref))
    for d in range(2):
        zf = z[:, (2 + d) * HG_W:(3 + d) * HG_W]
        lb = lb_all[d:d + 1, :]
        f = lb + (1.0 - lb) * jax.nn.sigmoid(zf)
        k = (1.0 - lb) * jax.nn.sigmoid(-zf)
        lf = jnp.log(f)
        lf_hi, lf_lo = _split2(lf)
        tri = _mask01(same & ((col <= row) if d == 0 else (col >= row)))
        b = _dot(tri, lf_hi) + _dot(tri, lf_lo)
        btot = _dot(tot, lf_hi) + _dot(tot, lf_lo)
        qd_ref, kd_ref, ke_ref = outs[d]
        qd_ref[...] = (qs * jnp.exp(b)).astype(BF16)
        kd_ref[...] = (k * jnp.exp(-b)).astype(BF16)
        ke_ref[...] = (k * jnp.exp(btot - b)).astype(BF16)
        dec_ref[:, d * HG_W:(d + 1) * HG_W] = jnp.exp(_dot(sel, lf_hi) + _dot(sel, lf_lo))


def _inproj_call(x2d, mod3, w_in_bf, lb_p, mod_row_fn):
    n = x2d.shape[0]
    nt = n // TOK_TILE
    tok_bf = lambda: jax.ShapeDtypeStruct((n, HG_W), BF16)
    tok_f = lambda: jax.ShapeDtypeStruct((n, HG_W), F32)
    tok_spec = pl.BlockSpec((TOK_TILE, HG_W), lambda t: (t, 0))
    gpt = TOK_TILE // GROUP
    cpt = TOK_TILE // CHUNK
    return pl.pallas_call(
        _inproj_kernel,
        grid=(nt,),
        in_specs=[
            pl.BlockSpec((TOK_TILE, D_MODEL), lambda t: (t, 0)),
            pl.BlockSpec((1, 1, 6 * D_MODEL), lambda t: (mod_row_fn(t), 0, 0)),
            pl.BlockSpec((D_MODEL, D_IN), lambda t: (0, 0)),
            pl.BlockSpec(lb_p.shape, lambda t: (0, 0, 0)),
        ],
        out_specs=[tok_spec] * 7 + [
            pl.BlockSpec((gpt, HG_W, GROUP), lambda t: (t, 0, 0)),
            pl.BlockSpec((cpt, 2 * HG_W), lambda t: (t, 0)),
            tok_spec, tok_spec, tok_spec,
        ],
        out_shape=[tok_bf() for _ in range(7)] + [
            jax.ShapeDtypeStruct((n // GROUP, HG_W, GROUP), BF16),
            jax.ShapeDtypeStruct((n // CHUNK, 2 * HG_W), F32),
            tok_f(), tok_f(), tok_f(),
        ],
        compiler_params=_cparams(1, 48 << 20),
        name="inproj",
    )(x2d, mod3, w_in_bf, lb_p)


def _hgrn_kernel(*refs, seq, has_init, emit_state):
    (qdf_ref, kdf_ref, kef_ref, qdb_ref, kdb_ref, keb_ref, v_ref, vt_ref, dec_ref, sg_ref,
     ng_ref) = refs[:11]
    pos = 11
    s0_ref = None
    if has_init:
        s0_ref = refs[pos]
        pos += 1
    og_ref = refs[pos]
    pos += 1
    st_ref = None
    if emit_state:
        st_ref = refs[pos]
        pos += 1
    oacc_ref = refs[pos]

    ngroups = seq // GROUP
    cpg = GROUP // CHUNK
    assert GPS * cpg == SUBLANES and ngroups % GPS == 0
    row = lax.broadcasted_iota(jnp.int32, (GROUP, GROUP), 0)
    col = lax.broadcasted_iota(jnp.int32, (GROUP, GROUP), 1)
    same = (row // CHUNK) == (col // CHUNK)
    dirs = ((qdf_ref, kdf_ref, kef_ref), (qdb_ref, kdb_ref, keb_ref))
    for d in range(2):
        qd_ref, kd_ref, ke_ref = dirs[d]
        keep = same & ((col <= row) if d == 0 else (col >= row))
        for h in range(HG_HEADS):
            cs = slice(h * HG_DK, (h + 1) * HG_DK)
            dcs = slice(d * HG_W + h * HG_DK, d * HG_W + (h + 1) * HG_DK)
            if has_init:
                st0 = s0_ref[0, d, h].T
            else:
                st0 = jnp.zeros((HG_DK, HG_DK), F32)

            def body(pi, st, d=d, cs=cs, dcs=dcs, qd_ref=qd_ref, kd_ref=kd_ref, ke_ref=ke_ref, keep=keep):
                pr = pi if d == 0 else ngroups // GPS - 1 - pi
                dec8 = dec_ref[pl.ds(pl.multiple_of(pr * SUBLANES, SUBLANES), SUBLANES), dcs]
                for gg in range(GPS):
                    sub = gg if d == 0 else GPS - 1 - gg
                    g = pr * GPS + sub
                    r0 = pl.multiple_of(g * GROUP, GROUP)
                    qd = qd_ref[pl.ds(r0, GROUP), cs]
                    kd = kd_ref[pl.ds(r0, GROUP), cs]
                    ke = ke_ref[pl.ds(r0, GROUP), cs]
                    vg = v_ref[pl.ds(r0, GROUP), cs]
                    vtg = vt_ref[g, cs, :]
                    att = jnp.where(keep, _dot_nt(qd, kd), 0.0).astype(BF16)
                    o = _dot(att, vg)
                    parts = [None] * cpg
                    for cc in range(cpg):
                        c = cc if d == 0 else cpg - 1 - cc
                        qc = qd[c * CHUNK:(c + 1) * CHUNK]
                        parts[c] = _dot_nt(qc, st.astype(BF16))
                        kem = ke * _mask01((row // CHUNK) == c)
                        dst = _dot(vtg, kem)
                        decay = dec8[sub * cpg + c:sub * cpg + c + 1, :]
                        st = decay * st + dst
                    o = o + jnp.concatenate(parts, axis=0)
                    if d == 0:
                        oacc_ref[pl.ds(r0, GROUP), cs] = o
                    else:
                        oacc_ref[pl.ds(r0, GROUP), cs] = oacc_ref[pl.ds(r0, GROUP), cs] + o
                return st

            st = lax.fori_loop(0, ngroups // GPS, body, st0)
            if emit_state:
                st_ref[0, d, h] = st.T

    for h in range(HG_HEADS):
        cs = slice(h * HG_DK, (h + 1) * HG_DK)
        o = oacc_ref[:, cs]
        ms = jnp.mean(o * o, -1, keepdims=True)
        on = o * lax.rsqrt(ms + RMS_EPS) * ng_ref[:, cs]
        og_ref[:, cs] = (on * sg_ref[:, cs]).astype(BF16)


def _hgrn_call(prep, norm_g, s0, seq, emit_state):
    qdf, kdf, kef, qdb, kdb, keb, v, vt, dec, sg = prep
    n = v.shape[0]
    nb = n // seq
    has_init = s0 is not None
    tok_spec = pl.BlockSpec((seq, HG_W), lambda b: (b, 0))
    in_specs = [tok_spec] * 7 + [
        pl.BlockSpec((seq // GROUP, HG_W, GROUP), lambda b: (b, 0, 0)),
        pl.BlockSpec((seq // CHUNK, 2 * HG_W), lambda b: (b, 0)),
        tok_spec,
        pl.BlockSpec((1, HG_W), lambda b: (0, 0)),
    ]
    args = [qdf, kdf, kef, qdb, kdb, keb, v, vt, dec, sg, norm_g]
    st_block = (1, 2, HG_HEADS, HG_DK, HG_DK)
    if has_init:
        in_specs.append(pl.BlockSpec(st_block, lambda b: (b, 0, 0, 0, 0)))
        args.append(s0)
    out_specs = [tok_spec]
    out_shape = [jax.ShapeDtypeStruct((n, HG_W), BF16)]
    if emit_state:
        out_specs.append(pl.BlockSpec(st_block, lambda b: (b, 0, 0, 0, 0)))
        out_shape.append(jax.ShapeDtypeStruct((nb, 2, HG_HEADS, HG_DK, HG_DK), F32))
    res = pl.pallas_call(
        functools.partial(_hgrn_kernel, seq=seq, has_init=has_init, emit_state=emit_state),
        grid=(nb,),
        in_specs=in_specs,
        out_specs=out_specs,
        out_shape=out_shape,
        scratch_shapes=[pltpu.VMEM((seq, HG_W), F32)],
        compiler_params=_cparams(1, 16 * seq * HG_W * 4 + (8 << 20)),
        name="hgrn",
    )(*args)
    return res if emit_state else (res[0], None)


def _shift_rows(x, off, nrows, row):
    if off == 0:
        return x
    y = pltpu.roll(x, (-off) % nrows, axis=0)
    ok = (row + off >= 0) & (row + off < nrows)
    return jnp.where(ok, y, 0.0)


def _scan_rows(a, u, reverse):
    n = a.shape[0]
    row = lax.broadcasted_iota(jnp.int32, a.shape, 0)
    s = 1
    while s < n:
        if reverse:
            a_sh = pltpu.roll(a, n - s, axis=0)
            u_sh = pltpu.roll(u, n - s, axis=0)
            ok = row < n - s
        else:
            a_sh = pltpu.roll(a, s, axis=0)
            u_sh = pltpu.roll(u, s, axis=0)
            ok = row >= s
        u = jnp.where(ok, a * u_sh + u, u)
        a = jnp.where(ok, a * a_sh, a)
        s *= 2
    return a, u


def _rglru_kernel(*refs, seq, grid_rows, has_init, emit_state):
    xr_ref, ggr_ref, cw_ref, cb_ref, wg_ref, bg_ref, lam_ref = refs[:7]
    pos = 7
    h0_ref = None
    if has_init:
        h0_ref = refs[pos]
        pos += 1
    y_ref = refs[pos]
    pos += 1
    st_ref = None
    if emit_state:
        st_ref = refs[pos]
        pos += 1
    a_ref, u_ref, hsum_ref = refs[pos:pos + 3]

    stride = GRID_W if grid_rows else 1
    row = lax.broadcasted_iota(jnp.int32, (seq, RG_W), 0)
    x = xr_ref[...]
    xc = jnp.zeros_like(x) + cb_ref[...]
    for j in range(CONV_W):
        xc = xc + cw_ref[j:j + 1, :] * _shift_rows(x, (j - CONV_LEFT) * stride, seq, row)
    gates = jax.nn.sigmoid(_dot(xc.astype(BF16), wg_ref[...]) + bg_ref[...])

    lam = lam_ref[...]
    nl = -lam
    softplus = jnp.maximum(nl, 0.0) + jnp.log1p(jnp.exp(-jnp.abs(nl)))

    for d in range(2):
        r = gates[:, (2 * d) * RG_W:(2 * d + 1) * RG_W]
        i = gates[:, (2 * d + 1) * RG_W:(2 * d + 2) * RG_W]
        log_a = -RG_C * r * softplus[d:d + 1, :]
        a = jnp.exp(log_a)
        u = jnp.sqrt(jnp.tanh(-log_a) * (a * a + 1.0)) * (i * xc)
        if has_init:
            h0 = h0_ref[0, d:d + 1, :]
        else:
            h0 = jnp.zeros((1, RG_W), F32)
        rev = d == 1
        if not grid_rows:
            acum, hz = _scan_rows(a, u, rev)
            h = hz + acum * h0
            last = h[0:1, :] if rev else h[seq - 1:seq, :]
        else:
            nrow = seq // GRID_W
            a_ref[...] = a
            u_ref[...] = u
            order = range(nrow - 1, -1, -1) if rev else range(nrow)
            hl = jnp.zeros((GRID_W, RG_W), F32)
            ac = jnp.ones((GRID_W, RG_W), F32)
            for rr in order:
                sl = slice(rr * GRID_W, (rr + 1) * GRID_W)
                ar = a_ref[sl, :]
                hl = ar * hl + u_ref[sl, :]
                ac = ar * ac
                u_ref[sl, :] = hl
                a_ref[sl, :] = ac
            ccum, hend0 = _scan_rows(ac, hl, rev)
            hend = hend0 + ccum * h0
            crow = lax.broadcasted_iota(jnp.int32, (GRID_W, RG_W), 0)
            if rev:
                hin = jnp.where(crow == GRID_W - 1, h0, pltpu.roll(hend, GRID_W - 1, axis=0))
                last = hend[0:1, :]
            else:
                hin = jnp.where(crow == 0, h0, pltpu.roll(hend, 1, axis=0))
                last = hend[GRID_W - 1:GRID_W, :]
            for rr in range(nrow):
                sl = slice(rr * GRID_W, (rr + 1) * GRID_W)
                u_ref[sl, :] = u_ref[sl, :] + a_ref[sl, :] * hin
            h = u_ref[...]
        if d == 0:
            hsum_ref[...] = h
        else:
            hsum_ref[...] = hsum_ref[...] + h
        if emit_state:
            st_ref[0, d:d + 1, :] = last
    y_ref[...] = (hsum_ref[...] * ggr_ref[...]).astype(BF16)


def _rglru_call(xr, ggr, conv_w, conv_b, wg, bg, lam, h0, seq, grid_rows, emit_state):
    n = xr.shape[0]
    nb = n // seq
    has_init = h0 is not None
    tok_spec = pl.BlockSpec((seq, RG_W), lambda b: (b, 0))
    full = lambda a: pl.BlockSpec(a.shape, lambda b: (0,) * a.ndim)
    in_specs = [tok_spec, tok_spec, full(conv_w), full(conv_b), full(wg), full(bg), full(lam)]
    args = [xr, ggr, conv_w, conv_b, wg, bg, lam]
    if has_init:
        in_specs.append(pl.BlockSpec((1, 2, RG_W), lambda b: (b, 0, 0)))
        args.append(h0)
    out_specs = [tok_spec]
    out_shape = [jax.ShapeDtypeStruct((n, RG_W), BF16)]
    if emit_state:
        out_specs.append(pl.BlockSpec((1, 2, RG_W), lambda b: (b, 0, 0)))
        out_shape.append(jax.ShapeDtypeStruct((nb, 2, RG_W), F32))
    res = pl.pallas_call(
        functools.partial(_rglru_kernel, seq=seq, grid_rows=grid_rows, has_init=has_init,
                          emit_state=emit_state),
        grid=(nb,),
        in_specs=in_specs,
        out_specs=out_specs,
        out_shape=out_shape,
        scratch_shapes=[pltpu.VMEM((seq, RG_W), F32)] * 3,
        compiler_params=_cparams(1, 20 * seq * RG_W * 4 + (8 << 20)),
        name="rglru",
    )(*args)
    return res if emit_state else (res[0], None)


def _oproj_kernel(og_ref, yrg_ref, x_ref, mod_ref, wo_ref, g_ref, b_ref, x1_ref, h2t_ref):
    mod = mod_ref[0]
    g1 = mod[:, 2 * D_MODEL:3 * D_MODEL]
    sh2 = mod[:, 3 * D_MODEL:4 * D_MODEL]
    sc2 = mod[:, 4 * D_MODEL:5 * D_MODEL]
    mix = _dot(og_ref[...], wo_ref[0:HG_W, :]) + _dot(yrg_ref[...], wo_ref[HG_W:, :])
    x1 = _layer_norm(ALPHA * x_ref[...] + g1 * mix, g_ref[...], b_ref[...])
    x1_ref[...] = x1
    h2 = x1 * (1.0 + sc2) + sh2
    h2t_ref[...] = h2.T.astype(BF16)


def _oproj_call(og, yrg, x2d, mod3, w_out_bf, ln_g, ln_b, mod_row_fn):
    n = x2d.shape[0]
    nt = n // TOK_TILE
    return pl.pallas_call(
        _oproj_kernel,
        grid=(nt,),
        in_specs=[
            pl.BlockSpec((TOK_TILE, HG_W), lambda t: (t, 0)),
            pl.BlockSpec((TOK_TILE, RG_W), lambda t: (t, 0)),
            pl.BlockSpec((TOK_TILE, D_MODEL), lambda t: (t, 0)),
            pl.BlockSpec((1, 1, 6 * D_MODEL), lambda t: (mod_row_fn(t), 0, 0)),
            pl.BlockSpec((D_MODEL, D_MODEL), lambda t: (0, 0)),
            pl.BlockSpec((1, D_MODEL), lambda t: (0, 0)),
            pl.BlockSpec((1, D_MODEL), lambda t: (0, 0)),
        ],
        out_specs=[
            pl.BlockSpec((TOK_TILE, D_MODEL), lambda t: (t, 0)),
            pl.BlockSpec((D_MODEL, TOK_TILE), lambda t: (0, t)),
        ],
        out_shape=[
            jax.ShapeDtypeStruct((n, D_MODEL), F32),
            jax.ShapeDtypeStruct((D_MODEL, n), BF16),
        ],
        compiler_params=_cparams(1, 32 << 20),
        name="oproj",
    )(og, yrg, x2d, mod3, w_out_bf, ln_g, ln_b)


def _cmpx(vals, i, j):
    hi = jnp.maximum(vals[i], vals[j])
    lo = jnp.minimum(vals[i], vals[j])
    vals[i] = hi
    vals[j] = lo


def _bitonic_merge_desc(vals):
    n = len(vals)
    d = n // 2
    while d >= 1:
        for i in range(n):
            if (i & d) == 0:
                _cmpx(vals, i, i + d)
        d //= 2
    return vals


def _sort_desc(vals):
    n = len(vals)
    if n == 1:
        return vals
    top = _sort_desc(vals[:n // 2])
    bot = _sort_desc(vals[n // 2:])
    return _bitonic_merge_desc(top + bot[::-1])


def _merge_top(a, b):
    n = len(a)
    return _bitonic_merge_desc([jnp.maximum(a[i], b[n - 1 - i]) for i in range(n)])


def _merge_sublanes(vals):
    s = SUBLANES // 2
    while s >= 1:
        rolled = [pltpu.roll(v, s, axis=0) for v in vals]
        vals = _merge_top(vals, rolled)
        s //= 2
    return vals


def _top_values(s):
    nv = PEER_NKEYS // SUBLANES
    vals = [s[i * SUBLANES:(i + 1) * SUBLANES, :] for i in range(nv)]
    vals = _sort_desc(vals)
    return _merge_sublanes(vals)


def _route_kernel(ht_ref, wqt_ref, keys_ref, s0_ref, s1_ref, e1_ref, c0_ref, tau_ref):
    tm = ht_ref.shape[1]
    qt = _dot(wqt_ref[...], ht_ref[...]).astype(BF16)
    half = PEER_DQ // 2
    sub = lax.broadcasted_iota(jnp.int32, (SUBLANES, tm), 0)
    for h in range(PEER_HEADS):
        s = []
        top = []
        for p in range(2):
            r0 = (2 * h + p) * half
            sp = _dot(keys_ref[2 * h + p], qt[r0:r0 + half, :])
            s.append(sp)
            top.append(_top_values(sp))
        lists = []
        for gb in range(PEER_TOPK // SUBLANES):
            bsel = jnp.zeros((SUBLANES, tm), F32)
            for b in range(SUBLANES):
                bsel = jnp.where(sub == b, top[1][gb * SUBLANES + b], bsel)
            lists.append([top[0][a] + bsel for a in range(PEER_TOPK)])
        cand = lists[0]
        for gb in range(1, len(lists)):
            cand = _merge_top(cand, lists[gb])
        fv = _merge_sublanes(cand)
        mx = fv[0][0:1, :]
        zsum = jnp.zeros((1, tm), F32)
        for kk in range(PEER_TOPK):
            zsum = zsum + jnp.exp(fv[kk][0:1, :] - mx)
        tau_ref[h:h + 1, :] = fv[PEER_TOPK - 1][0:1, :]
        s0_ref[h] = s[0]
        s1_ref[h] = s[1]
        e1_ref[h] = jnp.exp(s[1] - top[1][0][0:1, :])
        c0_ref[h] = jnp.exp(s[0] - top[0][0][0:1, :]) / zsum


def _route_call(h2t, wqt_bf, keys_bf):
    n = h2t.shape[1]
    nt = n // ROUTE_TM
    big = lambda: jax.ShapeDtypeStruct((PEER_HEADS, PEER_NKEYS, n), F32)
    big_spec = pl.BlockSpec((PEER_HEADS, PEER_NKEYS, ROUTE_TM), lambda t: (0, 0, t))
    return pl.pallas_call(
        _route_kernel,
        grid=(nt,),
        in_specs=[
            pl.BlockSpec((D_MODEL, ROUTE_TM), lambda t: (0, t)),
            pl.BlockSpec(wqt_bf.shape, lambda t: (0, 0)),
            pl.BlockSpec(keys_bf.shape, lambda t: (0, 0, 0)),
        ],
        out_specs=[big_spec] * 4 + [pl.BlockSpec((PEER_HEADS, ROUTE_TM), lambda t: (0, t))],
        out_shape=[big() for _ in range(4)] + [jax.ShapeDtypeStruct((PEER_HEADS, n), F32)],
        compiler_params=_cparams(1, 40 << 20),
        name="route",
    )(h2t, wqt_bf, keys_bf)


def _peer_kernel(ht_ref, u_ref, vt_ref, s0_ref, s1_ref, e1_ref, c0_ref, tau_ref,
                 x1_ref, mod_ref, g_ref, b_ref, out_ref, pt_ref, wt_ref, acc_ref):
    step = pl.program_id(1)
    nsteps = pl.num_programs(1)
    tm = ht_ref.shape[1]
    nloc = PEER_EB // PEER_NKEYS

    @pl.when(step == 0)
    def _():
        acc_ref[...] = jnp.zeros_like(acc_ref)

    pt_ref[...] = _dot(u_ref[...], ht_ref[...])

    ig0 = pl.multiple_of(step * nloc, nloc)
    for mt in range(tm // LANES):
        ls = slice(mt * LANES, (mt + 1) * LANES)
        for il in range(nloc):
            rs = slice(il * PEER_NKEYS, (il + 1) * PEER_NKEYS)
            gate = jnp.zeros((PEER_NKEYS, LANES), F32)
            for h in range(PEER_HEADS):
                s0row = s0_ref[h, pl.ds(ig0, nloc), ls][il:il + 1, :]
                c0row = c0_ref[h, pl.ds(ig0, nloc), ls][il:il + 1, :]
                tau = tau_ref[h:h + 1, ls]
                ssum = s1_ref[h, :, ls] + s0row
                gate = gate + jnp.where(ssum >= tau, e1_ref[h, :, ls], 0.0) * c0row
            act = _gelu(pt_ref[rs, ls])
            wt_ref[rs, ls] = (gate * act).astype(BF16)
    acc_ref[...] += _dot(vt_ref[...], wt_ref[...])

    @pl.when(step == nsteps - 1)
    def _():
        mod = mod_ref[0]
        g2 = mod[:, 5 * D_MODEL:6 * D_MODEL]
        ff = acc_ref[...].T
        out_ref[...] = _layer_norm(ALPHA * x1_ref[...] + g2 * ff, g_ref[...], b_ref[...])


def _peer_call(h2t, u_bf, vt_bf, route, x1, mod3, ln_g, ln_b, mod_row_fn):
    s0, s1, e1, c0, tau = route
    n = h2t.shape[1]
    nt = n // PEER_TM
    nsteps = u_bf.shape[0] // PEER_EB
    big_spec = pl.BlockSpec((PEER_HEADS, PEER_NKEYS, PEER_TM), lambda t, i: (0, 0, t))
    return pl.pallas_call(
        _peer_kernel,
        grid=(nt, nsteps),
        in_specs=[
            pl.BlockSpec((D_MODEL, PEER_TM), lambda t, i: (0, t)),
            pl.BlockSpec((PEER_EB, D_MODEL), lambda t, i: (i, 0)),
            pl.BlockSpec((D_MODEL, PEER_EB), lambda t, i: (0, i)),
            big_spec, big_spec, big_spec, big_spec,
            pl.BlockSpec((PEER_HEADS, PEER_TM), lambda t, i: (0, t)),
            pl.BlockSpec((PEER_TM, D_MODEL), lambda t, i: (t, 0)),
            pl.BlockSpec((1, 1, 6 * D_MODEL), lambda t, i: (mod_row_fn(t), 0, 0)),
            pl.BlockSpec((1, D_MODEL), lambda t, i: (0, 0)),
            pl.BlockSpec((1, D_MODEL), lambda t, i: (0, 0)),
        ],
        out_specs=pl.BlockSpec((PEER_TM, D_MODEL), lambda t, i: (t, 0)),
        out_shape=jax.ShapeDtypeStruct((n, D_MODEL), F32),
        scratch_shapes=[
            pltpu.VMEM((PEER_EB, PEER_TM), F32),
            pltpu.VMEM((PEER_EB, PEER_TM), BF16),
            pltpu.VMEM((D_MODEL, PEER_TM), F32),
        ],
        compiler_params=_cparams(2, 52 << 20),
        name="peer",
    )(h2t, u_bf, vt_bf, s0, s1, e1, c0, tau, x1, mod3, ln_g, ln_b)


def _block_diag(w):
    nh, bw, _ = w.shape
    eye = jnp.eye(nh, dtype=w.dtype)
    return (eye[:, None, :, None] * w[:, :, None, :]).reshape(nh * bw, nh * bw)


def _run_path(x, mod3, weights, s_hg0, s_rg0, seq, grid_rows, emit_state, mod_tok_row):
    bsz = x.shape[0]
    x2d = x.reshape(bsz * seq, D_MODEL)
    row256 = lambda t: mod_tok_row(t, TOK_TILE)
    prep = _inproj_call(x2d, mod3, weights["w_in"], weights["lb"], row256)
    qdf, kdf, kef, qdb, kdb, keb, v, vt, dec, sg, xr, ggr = prep
    og, st_hg = _hgrn_call((qdf, kdf, kef, qdb, kdb, keb, v, vt, dec, sg), weights["hg_norm"],
                           s_hg0, seq, emit_state)
    yrg, st_rg = _rglru_call(xr, ggr, weights["conv_w"], weights["conv_b"], weights["wg"],
                             weights["bg"], weights["lam"], s_rg0, seq, grid_rows, emit_state)
    x1, h2t = _oproj_call(og, yrg, x2d, mod3, weights["w_out"], weights["ln1_g"], weights["ln1_b"],
                          row256)
    route = _route_call(h2t, weights["wqt"], weights["keys"])
    out = _peer_call(h2t, weights["u"], weights["vt"], route, x1, mod3, weights["ln2_g"],
                     weights["ln2_b"], lambda t: mod_tok_row(t, PEER_TM))
    return out.reshape(bsz, seq, D_MODEL), st_hg, st_rg


def kernel(x_prompt, x_sample, c, state_hgrn, state_rglru, c_ctx, w_ada, b_ada, w_in, hgrn_lb,
           hgrn_norm_g, conv_w, conv_b, rg_wr, rg_br, rg_wi, rg_bi, rg_lam, w_out, ln1_g, ln1_b,
           peer_wq, peer_keys, peer_u, peer_v, ln2_g, ln2_b):
    assert w_ada.shape[0] == DEPTH
    bp, seq_p, _ = x_prompt.shape
    bs, seq_s, _ = x_sample.shape
    l = 0
    nrows = -(-(1 + bs) // SUBLANES) * SUBLANES
    cond = jnp.zeros((nrows, D_MODEL), F32).at[0].set(c_ctx).at[1:1 + bs].set(c)
    mod = _mod_call(cond, w_ada[l], b_ada[l][None, :])
    mod3 = mod[:, None, :]

    wg = jnp.concatenate([_block_diag(rg_wr[l, 0]), _block_diag(rg_wi[l, 0]),
                          _block_diag(rg_wr[l, 1]), _block_diag(rg_wi[l, 1])], axis=1)
    bg = jnp.concatenate([rg_br[l, 0], rg_bi[l, 0], rg_br[l, 1], rg_bi[l, 1]])[None, :]
    weights = {
        "w_in": w_in[l].astype(BF16),
        "lb": hgrn_lb,
        "hg_norm": hgrn_norm_g[l][None, :],
        "conv_w": conv_w[l],
        "conv_b": conv_b[l][None, :],
        "wg": wg.astype(BF16),
        "bg": bg,
        "lam": rg_lam[l],
        "w_out": w_out[l].astype(BF16),
        "ln1_g": ln1_g[l][None, :],
        "ln1_b": ln1_b[l][None, :],
        "wqt": peer_wq[l].T.astype(BF16),
        "keys": peer_keys[l].reshape(PEER_HEADS * 2, PEER_NKEYS, PEER_DQ // 2).astype(BF16),
        "u": peer_u[l].astype(BF16),
        "vt": peer_v[l].T.astype(BF16),
        "ln2_g": ln2_g[l][None, :],
        "ln2_b": ln2_b[l][None, :],
    }

    yp, st_hg, st_rg = _run_path(x_prompt, mod3, weights, None, None, seq_p, False, True,
                                 lambda t, tile: 0)
    ys, _, _ = _run_path(x_sample, mod3, weights, state_hgrn[:, l], state_rglru[:, l], seq_s, True,
                         False, lambda t, tile: 1 + (t * tile) // seq_s)
    new_hg = st_hg[:, None].astype(x_prompt.dtype)
    new_rg = st_rg[:, None].astype(x_prompt.dtype)
    return (yp, ys, new_hg, new_rg)
```

```python
import functools

import jax
import jax.numpy as jnp
from jax import lax
from jax.experimental import pallas as pl
from jax.experimental.pallas import tpu as pltpu

F32 = jnp.float32
BF16 = jnp.bfloat16

D_MODEL = 1024
HG_HEADS = 4
HG_DK = 128
HG_W = 512
RG_W = 512
RG_HEADS = 8
RG_BW = 64
RG_C = 8.0
CONV_W = 4
CONV_LEFT = CONV_W // 2
GRID_W = 64
CHUNK = 32
D_IN = 5 * HG_W + 2 * RG_W
PEER_HEADS = 8
PEER_NKEYS = 128
PEER_TOPK = 16
PEER_DQ = 256
DEPTH = 1
ALPHA = (2.0 * DEPTH) ** 0.25
LN_EPS = 1e-5
RMS_EPS = 1e-6

LANES = 128
SUBLANES = 8
VMEM_CAP_BYTES = 56 * 1024 * 1024

TOK_TILE = 256
GROUP = 128
GPS = SUBLANES * CHUNK // GROUP
PEER_TM = 512
PEER_EB = 1024
ROUTE_TM = 256


def _cparams(n_axes, vmem_bytes):
    return pltpu.CompilerParams(
        dimension_semantics=("arbitrary",) * n_axes,
        vmem_limit_bytes=min(int(vmem_bytes), VMEM_CAP_BYTES),
    )


def _silu(x):
    return x * jax.nn.sigmoid(x)


def _gelu(x):
    return jax.nn.gelu(x, approximate=True)


def _dot(a, b):
    return jnp.dot(a, b, preferred_element_type=F32)


def _dot_nt(a, b):
    return lax.dot_general(a, b, (((1,), (1,)), ((), ())), preferred_element_type=F32)


def _layer_norm(x, g, b):
    mu = jnp.mean(x, -1, keepdims=True)
    xc = x - mu
    var = jnp.mean(xc * xc, -1, keepdims=True)
    return xc * lax.rsqrt(var + LN_EPS) * g + b


def _mod_kernel(cond_ref, w_ref, b_ref, o_ref):
    c = _silu(cond_ref[...]).astype(BF16)
    o_ref[...] = _dot(c, w_ref[...].astype(BF16)) + b_ref[...]


def _mod_call(cond, w_ada, b_ada):
    rows = cond.shape[0]
    nblk = w_ada.shape[1] // D_MODEL
    return pl.pallas_call(
        _mod_kernel,
        grid=(nblk,),
        in_specs=[
            pl.BlockSpec((rows, D_MODEL), lambda j: (0, 0)),
            pl.BlockSpec((D_MODEL, D_MODEL), lambda j: (0, j)),
            pl.BlockSpec((1, D_MODEL), lambda j: (0, j)),
        ],
        out_specs=pl.BlockSpec((rows, D_MODEL), lambda j: (0, j)),
        out_shape=jax.ShapeDtypeStruct((rows, w_ada.shape[1]), F32),
        compiler_params=_cparams(1, 24 << 20),
        name="mod",
    )(cond, w_ada, b_ada)


def _mask01(cond):
    return jnp.where(cond, 1.0, 0.0).astype(BF16)


def _split2(x):
    hi = x.astype(BF16)
    lo = (x - hi.astype(F32)).astype(BF16)
    return hi, lo


def _inproj_kernel(x_ref, mod_ref, w_ref, lb_ref,
                   qdf_ref, kdf_ref, kef_ref, qdb_ref, kdb_ref, keb_ref,
                   v_ref, vt_ref, dec_ref, sg_ref, xr_ref, ggr_ref):
    tt = x_ref.shape[0]
    mod = mod_ref[0]
    sh1 = mod[:, 0:D_MODEL]
    sc1 = mod[:, D_MODEL:2 * D_MODEL]
    h = (x_ref[...] * (1.0 + sc1) + sh1).astype(BF16)
    z = _dot(h, w_ref[...])
    q = z[:, 0:HG_W]
    iv = z[:, HG_W:2 * HG_W]
    g = z[:, 4 * HG_W:5 * HG_W]
    xr = z[:, 5 * HG_W:5 * HG_W + RG_W]
    gr = z[:, 5 * HG_W + RG_W:]
    qs = _silu(q)
    ivb = iv.astype(BF16)
    v_ref[...] = ivb
    ivt = iv.T.astype(BF16)
    for gi in range(tt // GROUP):
        vt_ref[gi] = ivt[:, gi * GROUP:(gi + 1) * GROUP]
    sg_ref[...] = _silu(g)
    xr_ref[...] = xr
    ggr_ref[...] = _gelu(gr)

    lbp = lb_ref[...]
    mx = jnp.max(lbp, axis=1, keepdims=True)
    e = jnp.exp(lbp - mx)
    lb_all = e[:, 0, :] / jnp.sum(e, axis=1)

    row = lax.broadcasted_iota(jnp.int32, (tt, tt), 0)
    col = lax.broadcasted_iota(jnp.int32, (tt, tt), 1)
    same = (row // CHUNK) == (col // CHUNK)
    tot = _mask01(same)
    srow = lax.broadcasted_iota(jnp.int32, (tt // CHUNK, tt), 0)
    scol = lax.broadcasted_iota(jnp.int32, (tt // CHUNK, tt), 1)
    sel = _mask01(srow == scol // CHUNK)

    outs = ((qdf_ref, kdf_ref, kef_ref), (qdb_ref, kdb_ref, keb_ref))
    for d in range(2):
        zf = z[:, (2 + d) * HG_W:(3 + d) * HG_W]
        lb = lb_all[d:d + 1, :]
        f = lb + (1.0 - lb) * jax.nn.sigmoid(zf)
        k = (1.0 - lb) * jax.nn.sigmoid(-zf)
        lf = jnp.log(f)
        lf_hi, lf_lo = _split2(lf)
        tri = _mask01(same & ((col <= row) if d == 0 else (col >= row)))
        b = _dot(tri, lf_hi) + _dot(tri, lf_lo)
        btot = _dot(tot, lf_hi) + _dot(tot, lf_lo)
        qd_ref, kd_ref, ke_ref = outs[d]
        qd_ref[...] = (qs * jnp.exp(b)).astype(BF16)
        kd_ref[...] = (k * jnp.exp(-b)).astype(BF16)
        ke_ref[...] = (k * jnp.exp(btot - b)).astype(BF16)
        dec_ref[:, d * HG_W:(d + 1) * HG_W] = jnp.exp(_dot(sel, lf_hi) + _dot(sel, lf_lo))


def _inproj_call(x2d, mod3, w_in_bf, lb_p, mod_row_fn):
    n = x2d.shape[0]
    nt = n // TOK_TILE
    tok_bf = lambda: jax.ShapeDtypeStruct((n, HG_W), BF16)
    tok_f = lambda: jax.ShapeDtypeStruct((n, HG_W), F32)
    tok_spec = pl.BlockSpec((TOK_TILE, HG_W), lambda t: (t, 0))
    gpt = TOK_TILE // GROUP
    cpt = TOK_TILE // CHUNK
    return pl.pallas_call(
        _inproj_kernel,
        grid=(nt,),
        in_specs=[
            pl.BlockSpec((TOK_TILE, D_MODEL), lambda t: (t, 0)),
            pl.BlockSpec((1, 1, 6 * D_MODEL), lambda t: (mod_row_fn(t), 0, 0)),
            pl.BlockSpec((D_MODEL, D_IN), lambda t: (0, 0)),
            pl.BlockSpec(lb_p.shape, lambda t: (0, 0, 0)),
        ],
        out_specs=[tok_spec] * 7 + [
            pl.BlockSpec((gpt, HG_W, GROUP), lambda t: (t, 0, 0)),
            pl.BlockSpec((cpt, 2 * HG_W), lambda t: (t, 0)),
            tok_spec, tok_spec, tok_spec,
        ],
        out_shape=[tok_bf() for _ in range(7)] + [
            jax.ShapeDtypeStruct((n // GROUP, HG_W, GROUP), BF16),
            jax.ShapeDtypeStruct((n // CHUNK, 2 * HG_W), F32),
            tok_f(), tok_f(), tok_f(),
        ],
        compiler_params=_cparams(1, 48 << 20),
        name="inproj",
    )(x2d, mod3, w_in_bf, lb_p)


def _hgrn_kernel(*refs, seq, has_init, emit_state):
    (qdf_ref, kdf_ref, kef_ref, qdb_ref, kdb_ref, keb_ref, v_ref, vt_ref, dec_ref, sg_ref,
     ng_ref) = refs[:11]
    pos = 11
    s0_ref = None
    if has_init:
        s0_ref = refs[pos]
        pos += 1
    og_ref = refs[pos]
    pos += 1
    st_ref = None
    if emit_state:
        st_ref = refs[pos]
        pos += 1
    oacc_ref = refs[pos]

    ngroups = seq // GROUP
    cpg = GROUP // CHUNK
    assert GPS * cpg == SUBLANES and ngroups % GPS == 0
    row = lax.broadcasted_iota(jnp.int32, (GROUP, GROUP), 0)
    col = lax.broadcasted_iota(jnp.int32, (GROUP, GROUP), 1)
    same = (row // CHUNK) == (col // CHUNK)
    dirs = ((qdf_ref, kdf_ref, kef_ref), (qdb_ref, kdb_ref, keb_ref))
    for d in range(2):
        qd_ref, kd_ref, ke_ref = dirs[d]
        keep = same & ((col <= row) if d == 0 else (col >= row))
        for h in range(HG_HEADS):
            cs = slice(h * HG_DK, (h + 1) * HG_DK)
            dcs = slice(d * HG_W + h * HG_DK, d * HG_W + (h + 1) * HG_DK)
            if has_init:
                st0 = s0_ref[0, d, h].T
            else:
                st0 = jnp.zeros((HG_DK, HG_DK), F32)

            def body(pi, st, d=d, cs=cs, dcs=dcs, qd_ref=qd_ref, kd_ref=kd_ref, ke_ref=ke_ref, keep=keep):
                pr = pi if d == 0 else ngroups // GPS - 1 - pi
                dec8 = dec_ref[pl.ds(pl.multiple_of(pr * SUBLANES, SUBLANES), SUBLANES), dcs]
                for gg in range(GPS):
                    sub = gg if d == 0 else GPS - 1 - gg
                    g = pr * GPS + sub
                    r0 = pl.multiple_of(g * GROUP, GROUP)
                    qd = qd_ref[pl.ds(r0, GROUP), cs]
                    kd = kd_ref[pl.ds(r0, GROUP), cs]
                    ke = ke_ref[pl.ds(r0, GROUP), cs]
                    vg = v_ref[pl.ds(r0, GROUP), cs]
                    vtg = vt_ref[g, cs, :]
                    att = jnp.where(keep, _dot_nt(qd, kd), 0.0).astype(BF16)
                    o = _dot(att, vg)
                    parts = [None] * cpg
                    for cc in range(cpg):
                        c = cc if d == 0 else cpg - 1 - cc
                        qc = qd[c * CHUNK:(c + 1) * CHUNK]
                        parts[c] = _dot_nt(qc, st.astype(BF16))
                        kem = ke * _mask01((row // CHUNK) == c)
                        dst = _dot(vtg, kem)
                        decay = dec8[sub * cpg + c:sub * cpg + c + 1, :]
                        st = decay * st + dst
                    o = o + jnp.concatenate(parts, axis=0)
                    if d == 0:
                        oacc_ref[pl.ds(r0, GROUP), cs] = o
                    else:
                        oacc_ref[pl.ds(r0, GROUP), cs] = oacc_ref[pl.ds(r0, GROUP), cs] + o
                return st

            st = lax.fori_loop(0, ngroups // GPS, body, st0)
            if emit_state:
                st_ref[0, d, h] = st.T

    for h in range(HG_HEADS):
        cs = slice(h * HG_DK, (h + 1) * HG_DK)
        o = oacc_ref[:, cs]
        ms = jnp.mean(o * o, -1, keepdims=True)
        on = o * lax.rsqrt(ms + RMS_EPS) * ng_ref[:, cs]
        og_ref[:, cs] = (on * sg_ref[:, cs]).astype(BF16)


def _hgrn_call(prep, norm_g, s0, seq, emit_state):
    qdf, kdf, kef, qdb, kdb, keb, v, vt, dec, sg = prep
    n = v.shape[0]
    nb = n // seq
    has_init = s0 is not None
    tok_spec = pl.BlockSpec((seq, HG_W), lambda b: (b, 0))
    in_specs = [tok_spec] * 7 + [
        pl.BlockSpec((seq // GROUP, HG_W, GROUP), lambda b: (b, 0, 0)),
        pl.BlockSpec((seq // CHUNK, 2 * HG_W), lambda b: (b, 0)),
        tok_spec,
        pl.BlockSpec((1, HG_W), lambda b: (0, 0)),
    ]
    args = [qdf, kdf, kef, qdb, kdb, keb, v, vt, dec, sg, norm_g]
    st_block = (1, 2, HG_HEADS, HG_DK, HG_DK)
    if has_init:
        in_specs.append(pl.BlockSpec(st_block, lambda b: (b, 0, 0, 0, 0)))
        args.append(s0)
    out_specs = [tok_spec]
    out_shape = [jax.ShapeDtypeStruct((n, HG_W), BF16)]
    if emit_state:
        out_specs.append(pl.BlockSpec(st_block, lambda b: (b, 0, 0, 0, 0)))
        out_shape.append(jax.ShapeDtypeStruct((nb, 2, HG_HEADS, HG_DK, HG_DK), F32))
    res = pl.pallas_call(
        functools.partial(_hgrn_kernel, seq=seq, has_init=has_init, emit_state=emit_state),
        grid=(nb,),
        in_specs=in_specs,
        out_specs=out_specs,
        out_shape=out_shape,
        scratch_shapes=[pltpu.VMEM((seq, HG_W), F32)],
        compiler_params=_cparams(1, 16 * seq * HG_W * 4 + (8 << 20)),
        name="hgrn",
    )(*args)
    return res if emit_state else (res[0], None)


def _shift_rows(x, off, nrows, row):
    if off == 0:
        return x
    y = pltpu.roll(x, (-off) % nrows, axis=0)
    ok = (row + off >= 0) & (row + off < nrows)
    return jnp.where(ok, y, 0.0)


def _scan_rows(a, u, reverse):
    n = a.shape[0]
    row = lax.broadcasted_iota(jnp.int32, a.shape, 0)
    s = 1
    while s < n:
        if reverse:
            a_sh = pltpu.roll(a, n - s, axis=0)
            u_sh = pltpu.roll(u, n - s, axis=0)
            ok = row < n - s
        else:
            a_sh = pltpu.roll(a, s, axis=0)
            u_sh = pltpu.roll(u, s, axis=0)
            ok = row >= s
        u = jnp.where(ok, a * u_sh + u, u)
        a = jnp.where(ok, a * a_sh, a)
        s *= 2
    return a, u


def _rglru_kernel(*refs, seq, grid_rows, has_init, emit_state):
    xr_ref, ggr_ref, cw_ref, cb_ref, wg_ref, bg_ref, lam_ref = refs[:7]
    pos = 7
    h0_ref = None
    if has_init:
        h0_ref = refs[pos]
        pos += 1
    y_ref = refs[pos]
    pos += 1
    st_ref = None
    if emit_state:
        st_ref = refs[pos]
        pos += 1
    a_ref, u_ref, hsum_ref = refs[pos:pos + 3]

    stride = GRID_W if grid_rows else 1
    row = lax.broadcasted_iota(jnp.int32, (seq, RG_W), 0)
    x = xr_ref[...]
    xc = jnp.zeros_like(x) + cb_ref[...]
    for j in range(CONV_W):
        xc = xc + cw_ref[j:j + 1, :] * _shift_rows(x, (j - CONV_LEFT) * stride, seq, row)
    gates = jax.nn.sigmoid(_dot(xc.astype(BF16), wg_ref[...]) + bg_ref[...])

    lam = lam_ref[...]
    nl = -lam
    softplus = jnp.maximum(nl, 0.0) + jnp.log1p(jnp.exp(-jnp.abs(nl)))

    for d in range(2):
        r = gates[:, (2 * d) * RG_W:(2 * d + 1) * RG_W]
        i = gates[:, (2 * d + 1) * RG_W:(2 * d + 2) * RG_W]
        log_a = -RG_C * r * softplus[d:d + 1, :]
        a = jnp.exp(log_a)
        u = jnp.sqrt(jnp.tanh(-log_a) * (a * a + 1.0)) * (i * xc)
        if has_init:
            h0 = h0_ref[0, d:d + 1, :]
        else:
            h0 = jnp.zeros((1, RG_W), F32)
        rev = d == 1
        if not grid_rows:
            acum, hz = _scan_rows(a, u, rev)
            h = hz + acum * h0
            last = h[0:1, :] if rev else h[seq - 1:seq, :]
        else:
            nrow = seq // GRID_W
            a_ref[...] = a
            u_ref[...] = u
            order = range(nrow - 1, -1, -1) if rev else range(nrow)
            hl = jnp.zeros((GRID_W, RG_W), F32)
            ac = jnp.ones((GRID_W, RG_W), F32)
            for rr in order:
                sl = slice(rr * GRID_W, (rr + 1) * GRID_W)
                ar = a_ref[sl, :]
                hl = ar * hl + u_ref[sl, :]
                ac = ar * ac
                u_ref[sl, :] = hl
                a_ref[sl, :] = ac
            ccum, hend0 = _scan_rows(ac, hl, rev)
            hend = hend0 + ccum * h0
            crow = lax.broadcasted_iota(jnp.int32, (GRID_W, RG_W), 0)
            if rev:
                hin = jnp.where(crow == GRID_W - 1, h0, pltpu.roll(hend, GRID_W - 1, axis=0))
                last = hend[0:1, :]
            else:
                hin = jnp.where(crow == 0, h0, pltpu.roll(hend, 1, axis=0))
                last = hend[GRID_W - 1:GRID_W, :]
            for rr in range(nrow):
                sl = slice(rr * GRID_W, (rr + 1) * GRID_W)
                u_ref[sl, :] = u_ref[sl, :] + a_ref[sl, :] * hin
            h = u_ref[...]
        if d == 0:
            hsum_ref[...] = h
        else:
            hsum_ref[...] = hsum_ref[...] + h
        if emit_state:
            st_ref[0, d:d + 1, :] = last
    y_ref[...] = (hsum_ref[...] * ggr_ref[...]).astype(BF16)


def _rglru_call(xr, ggr, conv_w, conv_b, wg, bg, lam, h0, seq, grid_rows, emit_state):
    n = xr.shape[0]
    nb = n // seq
    has_init = h0 is not None
    tok_spec = pl.BlockSpec((seq, RG_W), lambda b: (b, 0))
    full = lambda a: pl.BlockSpec(a.shape, lambda b: (0,) * a.ndim)
    in_specs = [tok_spec, tok_spec, full(conv_w), full(conv_b), full(wg), full(bg), full(lam)]
    args = [xr, ggr, conv_w, conv_b, wg, bg, lam]
    if has_init:
        in_specs.append(pl.BlockSpec((1, 2, RG_W), lambda b: (b, 0, 0)))
        args.append(h0)
    out_specs = [tok_spec]
    out_shape = [jax.ShapeDtypeStruct((n, RG_W), BF16)]
    if emit_state:
        out_specs.append(pl.BlockSpec((1, 2, RG_W), lambda b: (b, 0, 0)))
        out_shape.append(jax.ShapeDtypeStruct((nb, 2, RG_W), F32))
    res = pl.pallas_call(
        functools.partial(_rglru_kernel, seq=seq, grid_rows=grid_rows, has_init=has_init,
                          emit_state=emit_state),
        grid=(nb,),
        in_specs=in_specs,
        out_specs=out_specs,
        out_shape=out_shape,
        scratch_shapes=[pltpu.VMEM((seq, RG_W), F32)] * 3,
        compiler_params=_cparams(1, 20 * seq * RG_W * 4 + (8 << 20)),
        name="rglru",
    )(*args)
    return res if emit_state else (res[0], None)


def _oproj_kernel(og_ref, yrg_ref, x_ref, mod_ref, wo_ref, g_ref, b_ref, x1_ref, h2t_ref):
    mod = mod_ref[0]
    g1 = mod[:, 2 * D_MODEL:3 * D_MODEL]
    sh2 = mod[:, 3 * D_MODEL:4 * D_MODEL]
    sc2 = mod[:, 4 * D_MODEL:5 * D_MODEL]
    mix = _dot(og_ref[...], wo_ref[0:HG_W, :]) + _dot(yrg_ref[...], wo_ref[HG_W:, :])
    x1 = _layer_norm(ALPHA * x_ref[...] + g1 * mix, g_ref[...], b_ref[...])
    x1_ref[...] = x1
    h2 = x1 * (1.0 + sc2) + sh2
    h2t_ref[...] = h2.T.astype(BF16)


def _oproj_call(og, yrg, x2d, mod3, w_out_bf, ln_g, ln_b, mod_row_fn):
    n = x2d.shape[0]
    nt = n // TOK_TILE
    return pl.pallas_call(
        _oproj_kernel,
        grid=(nt,),
        in_specs=[
            pl.BlockSpec((TOK_TILE, HG_W), lambda t: (t, 0)),
            pl.BlockSpec((TOK_TILE, RG_W), lambda t: (t, 0)),
            pl.BlockSpec((TOK_TILE, D_MODEL), lambda t: (t, 0)),
            pl.BlockSpec((1, 1, 6 * D_MODEL), lambda t: (mod_row_fn(t), 0, 0)),
            pl.BlockSpec((D_MODEL, D_MODEL), lambda t: (0, 0)),
            pl.BlockSpec((1, D_MODEL), lambda t: (0, 0)),
            pl.BlockSpec((1, D_MODEL), lambda t: (0, 0)),
        ],
        out_specs=[
            pl.BlockSpec((TOK_TILE, D_MODEL), lambda t: (t, 0)),
            pl.BlockSpec((D_MODEL, TOK_TILE), lambda t: (0, t)),
        ],
        out_shape=[
            jax.ShapeDtypeStruct((n, D_MODEL), F32),
            jax.ShapeDtypeStruct((D_MODEL, n), BF16),
        ],
        compiler_params=_cparams(1, 32 << 20),
        name="oproj",
    )(og, yrg, x2d, mod3, w_out_bf, ln_g, ln_b)


def _cmpx(vals, i, j):
    hi = jnp.maximum(vals[i], vals[j])
    lo = jnp.minimum(vals[i], vals[j])
    vals[i] = hi
    vals[j] = lo


def _bitonic_merge_desc(vals):
    n = len(vals)
    d = n // 2
    while d >= 1:
        for i in range(n):
            if (i & d) == 0:
                _cmpx(vals, i, i + d)
        d //= 2
    return vals


def _sort_desc(vals):
    n = len(vals)
    if n == 1:
        return vals
    top = _sort_desc(vals[:n // 2])
    bot = _sort_desc(vals[n // 2:])
    return _bitonic_merge_desc(top + bot[::-1])


def _merge_top(a, b):
    n = len(a)
    return _bitonic_merge_desc([jnp.maximum(a[i], b[n - 1 - i]) for i in range(n)])


def _merge_sublanes(vals):
    s = SUBLANES // 2
    while s >= 1:
        rolled = [pltpu.roll(v, s, axis=0) for v in vals]
        vals = _merge_top(vals, rolled)
        s //= 2
    return vals


def _top_values(s):
    nv = PEER_NKEYS // SUBLANES
    vals = [s[i * SUBLANES:(i + 1) * SUBLANES, :] for i in range(nv)]
    vals = _sort_desc(vals)
    return _merge_sublanes(vals)


def _route_kernel(ht_ref, wqt_ref, keys_ref, nb_ref, c0_ref, r1_ref, e1_ref):
    tm = ht_ref.shape[1]
    qt = _dot(wqt_ref[...], ht_ref[...]).astype(BF16)
    half = PEER_DQ // 2
    sub = lax.broadcasted_iota(jnp.int32, (SUBLANES, tm), 0)
    for h in range(PEER_HEADS):
        s = []
        top = []
        for p in range(2):
            r0 = (2 * h + p) * half
            sp = _dot(keys_ref[2 * h + p], qt[r0:r0 + half, :])
            s.append(sp)
            top.append(_top_values(sp))
        lists = []
        for gb in range(PEER_TOPK // SUBLANES):
            bsel = jnp.zeros((SUBLANES, tm), F32)
            for b in range(SUBLANES):
                bsel = jnp.where(sub == b, top[1][gb * SUBLANES + b], bsel)
            lists.append([top[0][a] + bsel for a in range(PEER_TOPK)])
        cand = lists[0]
        for gb in range(1, len(lists)):
            cand = _merge_top(cand, lists[gb])
        fv = _merge_sublanes(cand)
        mx = fv[0][0:1, :]
        zsum = jnp.zeros((1, tm), F32)
        for kk in range(PEER_TOPK):
            zsum = zsum + jnp.exp(fv[kk][0:1, :] - mx)
        tau = fv[PEER_TOPK - 1][0:1, :]
        nb = jnp.zeros((PEER_NKEYS, tm), F32)
        rank1 = jnp.zeros((PEER_NKEYS, tm), F32)
        for b in range(PEER_TOPK):
            svb = top[1][b][0:1, :]
            nb = nb + jnp.where(s[0] + svb >= tau, 1.0, 0.0)
            rank1 = rank1 + jnp.where(svb > s[1], 1.0, 0.0)
        nb_ref[h] = nb
        r1_ref[h] = rank1.astype(BF16)
        e1_ref[h] = jnp.exp(s[1] - top[1][0][0:1, :]).astype(BF16)
        c0_ref[h] = jnp.exp(s[0] - top[0][0][0:1, :]) / zsum


def _route_call(h2t, wqt_bf, keys_bf):
    n = h2t.shape[1]
    nt = n // ROUTE_TM
    big = lambda dt: jax.ShapeDtypeStruct((PEER_HEADS, PEER_NKEYS, n), dt)
    big_spec = pl.BlockSpec((PEER_HEADS, PEER_NKEYS, ROUTE_TM), lambda t: (0, 0, t))
    return pl.pallas_call(
        _route_kernel,
        grid=(nt,),
        in_specs=[
            pl.BlockSpec((D_MODEL, ROUTE_TM), lambda t: (0, t)),
            pl.BlockSpec(wqt_bf.shape, lambda t: (0, 0)),
            pl.BlockSpec(keys_bf.shape, lambda t: (0, 0, 0)),
        ],
        out_specs=[big_spec] * 4,
        out_shape=[big(F32), big(F32), big(BF16), big(BF16)],
        compiler_params=_cparams(1, 40 << 20),
        name="route",
    )(h2t, wqt_bf, keys_bf)


def _peer_kernel(ht_ref, u_ref, vt_ref, nb_ref, c0_ref, r1_ref, e1_ref,
                 x1_ref, mod_ref, g_ref, b_ref, out_ref, pt_ref, wt_ref, acc_ref, r1s_ref, e1s_ref):
    step = pl.program_id(1)
    nsteps = pl.num_programs(1)
    tm = ht_ref.shape[1]
    nloc = PEER_EB // PEER_NKEYS

    @pl.when(step == 0)
    def _():
        acc_ref[...] = jnp.zeros_like(acc_ref)
        r1s_ref[...] = r1_ref[...]
        e1s_ref[...] = e1_ref[...]

    pt_ref[...] = _dot(u_ref[...], ht_ref[...])

    ig0 = pl.multiple_of(step * nloc, nloc)
    for mt in range(tm // LANES):
        ls = slice(mt * LANES, (mt + 1) * LANES)
        for il in range(nloc):
            rs = slice(il * PEER_NKEYS, (il + 1) * PEER_NKEYS)
            gate = jnp.zeros((PEER_NKEYS, LANES), BF16)
            zero = jnp.zeros((PEER_NKEYS, LANES), BF16)
            for h in range(PEER_HEADS):
                nbrow = nb_ref[h, pl.ds(ig0, nloc), ls][il:il + 1, :].astype(BF16)
                c0row = c0_ref[h, pl.ds(ig0, nloc), ls][il:il + 1, :].astype(BF16)
                sel = jnp.where(r1s_ref[h, :, ls] < nbrow, e1s_ref[h, :, ls], zero)
                gate = gate + sel * c0row
            act = _gelu(pt_ref[rs, ls])
            wt_ref[rs, ls] = (gate.astype(F32) * act).astype(BF16)
    acc_ref[...] += _dot(vt_ref[...], wt_ref[...])

    @pl.when(step == nsteps - 1)
    def _():
        mod = mod_ref[0]
        g2 = mod[:, 5 * D_MODEL:6 * D_MODEL]
        ff = acc_ref[...].T
        out_ref[...] = _layer_norm(ALPHA * x1_ref[...] + g2 * ff, g_ref[...], b_ref[...])


def _peer_call(h2t, u_bf, vt_bf, route, x1, mod3, ln_g, ln_b, mod_row_fn):
    nb, c0, r1, e1 = route
    n = h2t.shape[1]
    nt = n // PEER_TM
    nsteps = u_bf.shape[0] // PEER_EB
    big_spec = pl.BlockSpec((PEER_HEADS, PEER_NKEYS, PEER_TM), lambda t, i: (0, 0, t))
    return pl.pallas_call(
        _peer_kernel,
        grid=(nt, nsteps),
        in_specs=[
            pl.BlockSpec((D_MODEL, PEER_TM), lambda t, i: (0, t)),
            pl.BlockSpec((PEER_EB, D_MODEL), lambda t, i: (i, 0)),
            pl.BlockSpec((D_MODEL, PEER_EB), lambda t, i: (0, i)),
            big_spec, big_spec, big_spec, big_spec,
            pl.BlockSpec((PEER_TM, D_MODEL), lambda t, i: (t, 0)),
            pl.BlockSpec((1, 1, 6 * D_MODEL), lambda t, i: (mod_row_fn(t), 0, 0)),
            pl.BlockSpec((1, D_MODEL), lambda t, i: (0, 0)),
            pl.BlockSpec((1, D_MODEL), lambda t, i: (0, 0)),
        ],
        out_specs=pl.BlockSpec((PEER_TM, D_MODEL), lambda t, i: (t, 0)),
        out_shape=jax.ShapeDtypeStruct((n, D_MODEL), F32),
        scratch_shapes=[
            pltpu.VMEM((PEER_EB, PEER_TM), F32),
            pltpu.VMEM((PEER_EB, PEER_TM), BF16),
            pltpu.VMEM((D_MODEL, PEER_TM), F32),
            pltpu.VMEM((PEER_HEADS, PEER_NKEYS, PEER_TM), BF16),
            pltpu.VMEM((PEER_HEADS, PEER_NKEYS, PEER_TM), BF16),
        ],
        compiler_params=_cparams(2, 52 << 20),
        name="peer",
    )(h2t, u_bf, vt_bf, nb, c0, r1, e1, x1, mod3, ln_g, ln_b)


def _block_diag(w):
    nh, bw, _ = w.shape
    eye = jnp.eye(nh, dtype=w.dtype)
    return (eye[:, None, :, None] * w[:, :, None, :]).reshape(nh * bw, nh * bw)


def _run_path(x, mod3, weights, s_hg0, s_rg0, seq, grid_rows, emit_state, mod_tok_row):
    bsz = x.shape[0]
    x2d = x.reshape(bsz * seq, D_MODEL)
    row256 = lambda t: mod_tok_row(t, TOK_TILE)
    prep = _inproj_call(x2d, mod3, weights["w_in"], weights["lb"], row256)
    qdf, kdf, kef, qdb, kdb, keb, v, vt, dec, sg, xr, ggr = prep
    og, st_hg = _hgrn_call((qdf, kdf, kef, qdb, kdb, keb, v, vt, dec, sg), weights["hg_norm"],
                           s_hg0, seq, emit_state)
    yrg, st_rg = _rglru_call(xr, ggr, weights["conv_w"], weights["conv_b"], weights["wg"],
                             weights["bg"], weights["lam"], s_rg0, seq, grid_rows, emit_state)
    x1, h2t = _oproj_call(og, yrg, x2d, mod3, weights["w_out"], weights["ln1_g"], weights["ln1_b"],
                          row256)
    route = _route_call(h2t, weights["wqt"], weights["keys"])
    out = _peer_call(h2t, weights["u"], weights["vt"], route, x1, mod3, weights["ln2_g"],
                     weights["ln2_b"], lambda t: mod_tok_row(t, PEER_TM))
    return out.reshape(bsz, seq, D_MODEL), st_hg, st_rg


def kernel(x_prompt, x_sample, c, state_hgrn, state_rglru, c_ctx, w_ada, b_ada, w_in, hgrn_lb,
           hgrn_norm_g, conv_w, conv_b, rg_wr, rg_br, rg_wi, rg_bi, rg_lam, w_out, ln1_g, ln1_b,
           peer_wq, peer_keys, peer_u, peer_v, ln2_g, ln2_b):
    assert w_ada.shape[0] == DEPTH
    bp, seq_p, _ = x_prompt.shape
    bs, seq_s, _ = x_sample.shape
    l = 0
    nrows = -(-(1 + bs) // SUBLANES) * SUBLANES
    cond = jnp.zeros((nrows, D_MODEL), F32).at[0].set(c_ctx).at[1:1 + bs].set(c)
    mod = _mod_call(cond, w_ada[l], b_ada[l][None, :])
    mod3 = mod[:, None, :]

    wg = jnp.concatenate([_block_diag(rg_wr[l, 0]), _block_diag(rg_wi[l, 0]),
                          _block_diag(rg_wr[l, 1]), _block_diag(rg_wi[l, 1])], axis=1)
    bg = jnp.concatenate([rg_br[l, 0], rg_bi[l, 0], rg_br[l, 1], rg_bi[l, 1]])[None, :]
    weights = {
        "w_in": w_in[l].astype(BF16),
        "lb": hgrn_lb,
        "hg_norm": hgrn_norm_g[l][None, :],
        "conv_w": conv_w[l],
        "conv_b": conv_b[l][None, :],
        "wg": wg.astype(BF16),
        "bg": bg,
        "lam": rg_lam[l],
        "w_out": w_out[l].astype(BF16),
        "ln1_g": ln1_g[l][None, :],
        "ln1_b": ln1_b[l][None, :],
        "wqt": peer_wq[l].T.astype(BF16),
        "keys": peer_keys[l].reshape(PEER_HEADS * 2, PEER_NKEYS, PEER_DQ // 2).astype(BF16),
        "u": peer_u[l].astype(BF16),
        "vt": peer_v[l].T.astype(BF16),
        "ln2_g": ln2_g[l][None, :],
        "ln2_b": ln2_b[l][None, :],
    }

    yp, st_hg, st_rg = _run_path(x_prompt, mod3, weights, None, None, seq_p, False, True,
                                 lambda t, tile: 0)
    ys, _, _ = _run_path(x_sample, mod3, weights, state_hgrn[:, l], state_rglru[:, l], seq_s, True,
                         False, lambda t, tile: 1 + (t * tile) // seq_s)
    new_hg = st_hg[:, None].astype(x_prompt.dtype)
    new_rg = st_rg[:, None].astype(x_prompt.dtype)
    return (yp, ys, new_hg, new_rg)
```

```python
import functools

import jax
import jax.numpy as jnp
from jax import lax
from jax.experimental import pallas as pl
from jax.experimental.pallas import tpu as pltpu

F32 = jnp.float32
BF16 = jnp.bfloat16

D_MODEL = 1024
HG_HEADS = 4
HG_DK = 128
HG_W = 512
RG_W = 512
RG_HEADS = 8
RG_BW = 64
RG_C = 8.0
CONV_W = 4
CONV_LEFT = CONV_W // 2
GRID_W = 64
CHUNK = 32
D_IN = 5 * HG_W + 2 * RG_W
PEER_HEADS = 8
PEER_NKEYS = 128
PEER_TOPK = 16
PEER_DQ = 256
DEPTH = 1
ALPHA = (2.0 * DEPTH) ** 0.25
LN_EPS = 1e-5
RMS_EPS = 1e-6

LANES = 128
SUBLANES = 8
VMEM_CAP_BYTES = 56 * 1024 * 1024

TOK_TILE = 256
GROUP = 128
GPS = SUBLANES * CHUNK // GROUP
PEER_TM = 512
PEER_EB = 1024
ROUTE_TM = 256


def _cparams(n_axes, vmem_bytes, flags=None):
    return pltpu.CompilerParams(
        dimension_semantics=("arbitrary",) * n_axes,
        vmem_limit_bytes=min(int(vmem_bytes), VMEM_CAP_BYTES),
        flags=flags,
    )


def _silu(x):
    return x * jax.nn.sigmoid(x)


def _gelu(x):
    return jax.nn.gelu(x, approximate=True)


def _dot(a, b):
    return jnp.dot(a, b, preferred_element_type=F32)


def _dot_nt(a, b):
    return lax.dot_general(a, b, (((1,), (1,)), ((), ())), preferred_element_type=F32)


def _layer_norm(x, g, b):
    mu = jnp.mean(x, -1, keepdims=True)
    xc = x - mu
    var = jnp.mean(xc * xc, -1, keepdims=True)
    return xc * lax.rsqrt(var + LN_EPS) * g + b


def _mod_kernel(cond_ref, w_ref, b_ref, o_ref):
    c = _silu(cond_ref[...]).astype(BF16)
    o_ref[...] = _dot(c, w_ref[...].astype(BF16)) + b_ref[...]


def _mod_call(cond, w_ada, b_ada):
    rows = cond.shape[0]
    nblk = w_ada.shape[1] // D_MODEL
    return pl.pallas_call(
        _mod_kernel,
        grid=(nblk,),
        in_specs=[
            pl.BlockSpec((rows, D_MODEL), lambda j: (0, 0)),
            pl.BlockSpec((D_MODEL, D_MODEL), lambda j: (0, j)),
            pl.BlockSpec((1, D_MODEL), lambda j: (0, j)),
        ],
        out_specs=pl.BlockSpec((rows, D_MODEL), lambda j: (0, j)),
        out_shape=jax.ShapeDtypeStruct((rows, w_ada.shape[1]), F32),
        compiler_params=_cparams(1, 24 << 20),
        name="mod",
    )(cond, w_ada, b_ada)


def _mask01(cond):
    return jnp.where(cond, 1.0, 0.0).astype(BF16)


def _split2(x):
    hi = x.astype(BF16)
    lo = (x - hi.astype(F32)).astype(BF16)
    return hi, lo


def _inproj_kernel(x_ref, mod_ref, w_ref, lb_ref,
                   qdf_ref, kdf_ref, kef_ref, qdb_ref, kdb_ref, keb_ref,
                   v_ref, vt_ref, dec_ref, sg_ref, xr_ref, ggr_ref):
    tt = x_ref.shape[0]
    mod = mod_ref[0]
    sh1 = mod[:, 0:D_MODEL]
    sc1 = mod[:, D_MODEL:2 * D_MODEL]
    h = (x_ref[...] * (1.0 + sc1) + sh1).astype(BF16)
    z = _dot(h, w_ref[...])
    q = z[:, 0:HG_W]
    iv = z[:, HG_W:2 * HG_W]
    g = z[:, 4 * HG_W:5 * HG_W]
    xr = z[:, 5 * HG_W:5 * HG_W + RG_W]
    gr = z[:, 5 * HG_W + RG_W:]
    qs = _silu(q)
    ivb = iv.astype(BF16)
    v_ref[...] = ivb
    ivt = iv.T.astype(BF16)
    for gi in range(tt // GROUP):
        vt_ref[gi] = ivt[:, gi * GROUP:(gi + 1) * GROUP]
    sg_ref[...] = _silu(g)
    xr_ref[...] = xr
    ggr_ref[...] = _gelu(gr)

    lbp = lb_ref[...]
    mx = jnp.max(lbp, axis=1, keepdims=True)
    e = jnp.exp(lbp - mx)
    lb_all = e[:, 0, :] / jnp.sum(e, axis=1)

    row = lax.broadcasted_iota(jnp.int32, (tt, tt), 0)
    col = lax.broadcasted_iota(jnp.int32, (tt, tt), 1)
    same = (row // CHUNK) == (col // CHUNK)
    tot = _mask01(same)
    srow = lax.broadcasted_iota(jnp.int32, (tt // CHUNK, tt), 0)
    scol = lax.broadcasted_iota(jnp.int32, (tt // CHUNK, tt), 1)
    sel = _mask01(srow == scol // CHUNK)

    outs = ((qdf_ref, kdf_ref, kef_ref), (qdb_ref, kdb_ref, keb_ref))
    for d in range(2):
        zf = z[:, (2 + d) * HG_W:(3 + d) * HG_W]
        lb = lb_all[d:d + 1, :]
        f = lb + (1.0 - lb) * jax.nn.sigmoid(zf)
        k = (1.0 - lb) * jax.nn.sigmoid(-zf)
        lf = jnp.log(f)
        lf_hi, lf_lo = _split2(lf)
        tri = _mask01(same & ((col <= row) if d == 0 else (col >= row)))
        b = _dot(tri, lf_hi) + _dot(tri, lf_lo)
        btot = _dot(tot, lf_hi) + _dot(tot, lf_lo)
        qd_ref, kd_ref, ke_ref = outs[d]
        qd_ref[...] = (qs * jnp.exp(b)).astype(BF16)
        kd_ref[...] = (k * jnp.exp(-b)).astype(BF16)
        ke_ref[...] = (k * jnp.exp(btot - b)).astype(BF16)
        dec_ref[:, d * HG_W:(d + 1) * HG_W] = jnp.exp(_dot(sel, lf_hi) + _dot(sel, lf_lo))


def _inproj_call(x2d, mod3, w_in_bf, lb_p, mod_row_fn):
    n = x2d.shape[0]
    nt = n // TOK_TILE
    tok_bf = lambda: jax.ShapeDtypeStruct((n, HG_W), BF16)
    tok_f = lambda: jax.ShapeDtypeStruct((n, HG_W), F32)
    tok_spec = pl.BlockSpec((TOK_TILE, HG_W), lambda t: (t, 0))
    gpt = TOK_TILE // GROUP
    cpt = TOK_TILE // CHUNK
    return pl.pallas_call(
        _inproj_kernel,
        grid=(nt,),
        in_specs=[
            pl.BlockSpec((TOK_TILE, D_MODEL), lambda t: (t, 0)),
            pl.BlockSpec((1, 1, 6 * D_MODEL), lambda t: (mod_row_fn(t), 0, 0)),
            pl.BlockSpec((D_MODEL, D_IN), lambda t: (0, 0)),
            pl.BlockSpec(lb_p.shape, lambda t: (0, 0, 0)),
        ],
        out_specs=[tok_spec] * 7 + [
            pl.BlockSpec((gpt, HG_W, GROUP), lambda t: (t, 0, 0)),
            pl.BlockSpec((cpt, 2 * HG_W), lambda t: (t, 0)),
            tok_spec, tok_spec, tok_spec,
        ],
        out_shape=[tok_bf() for _ in range(7)] + [
            jax.ShapeDtypeStruct((n // GROUP, HG_W, GROUP), BF16),
            jax.ShapeDtypeStruct((n // CHUNK, 2 * HG_W), F32),
            tok_f(), tok_f(), tok_f(),
        ],
        compiler_params=_cparams(1, 48 << 20),
        name="inproj",
    )(x2d, mod3, w_in_bf, lb_p)


def _hgrn_kernel(*refs, seq, has_init, emit_state):
    (qdf_ref, kdf_ref, kef_ref, qdb_ref, kdb_ref, keb_ref, v_ref, vt_ref, dec_ref, sg_ref,
     ng_ref) = refs[:11]
    pos = 11
    s0_ref = None
    if has_init:
        s0_ref = refs[pos]
        pos += 1
    og_ref = refs[pos]
    pos += 1
    st_ref = None
    if emit_state:
        st_ref = refs[pos]
        pos += 1
    oacc_ref = refs[pos]

    ngroups = seq // GROUP
    cpg = GROUP // CHUNK
    assert GPS * cpg == SUBLANES and ngroups % GPS == 0
    row = lax.broadcasted_iota(jnp.int32, (GROUP, GROUP), 0)
    col = lax.broadcasted_iota(jnp.int32, (GROUP, GROUP), 1)
    same = (row // CHUNK) == (col // CHUNK)
    dirs = ((qdf_ref, kdf_ref, kef_ref), (qdb_ref, kdb_ref, keb_ref))
    for d in range(2):
        qd_ref, kd_ref, ke_ref = dirs[d]
        keep = same & ((col <= row) if d == 0 else (col >= row))
        for h in range(HG_HEADS):
            cs = slice(h * HG_DK, (h + 1) * HG_DK)
            dcs = slice(d * HG_W + h * HG_DK, d * HG_W + (h + 1) * HG_DK)
            if has_init:
                st0 = s0_ref[0, d, h].T
            else:
                st0 = jnp.zeros((HG_DK, HG_DK), F32)

            def body(pi, st, d=d, cs=cs, dcs=dcs, qd_ref=qd_ref, kd_ref=kd_ref, ke_ref=ke_ref, keep=keep):
                pr = pi if d == 0 else ngroups // GPS - 1 - pi
                dec8 = dec_ref[pl.ds(pl.multiple_of(pr * SUBLANES, SUBLANES), SUBLANES), dcs]
                for gg in range(GPS):
                    sub = gg if d == 0 else GPS - 1 - gg
                    g = pr * GPS + sub
                    r0 = pl.multiple_of(g * GROUP, GROUP)
                    qd = qd_ref[pl.ds(r0, GROUP), cs]
                    kd = kd_ref[pl.ds(r0, GROUP), cs]
                    ke = ke_ref[pl.ds(r0, GROUP), cs]
                    vg = v_ref[pl.ds(r0, GROUP), cs]
                    vtg = vt_ref[g, cs, :]
                    att = jnp.where(keep, _dot_nt(qd, kd), 0.0).astype(BF16)
                    o = _dot(att, vg)
                    parts = [None] * cpg
                    for cc in range(cpg):
                        c = cc if d == 0 else cpg - 1 - cc
                        qc = qd[c * CHUNK:(c + 1) * CHUNK]
                        parts[c] = _dot_nt(qc, st.astype(BF16))
                        kem = ke * _mask01((row // CHUNK) == c)
                        dst = _dot(vtg, kem)
                        decay = dec8[sub * cpg + c:sub * cpg + c + 1, :]
                        st = decay * st + dst
                    o = o + jnp.concatenate(parts, axis=0)
                    if d == 0:
                        oacc_ref[pl.ds(r0, GROUP), cs] = o
                    else:
                        oacc_ref[pl.ds(r0, GROUP), cs] = oacc_ref[pl.ds(r0, GROUP), cs] + o
                return st

            st = lax.fori_loop(0, ngroups // GPS, body, st0)
            if emit_state:
                st_ref[0, d, h] = st.T

    for h in range(HG_HEADS):
        cs = slice(h * HG_DK, (h + 1) * HG_DK)
        o = oacc_ref[:, cs]
        ms = jnp.mean(o * o, -1, keepdims=True)
        on = o * lax.rsqrt(ms + RMS_EPS) * ng_ref[:, cs]
        og_ref[:, cs] = (on * sg_ref[:, cs]).astype(BF16)


def _hgrn_call(prep, norm_g, s0, seq, emit_state):
    qdf, kdf, kef, qdb, kdb, keb, v, vt, dec, sg = prep
    n = v.shape[0]
    nb = n // seq
    has_init = s0 is not None
    tok_spec = pl.BlockSpec((seq, HG_W), lambda b: (b, 0))
    in_specs = [tok_spec] * 7 + [
        pl.BlockSpec((seq // GROUP, HG_W, GROUP), lambda b: (b, 0, 0)),
        pl.BlockSpec((seq // CHUNK, 2 * HG_W), lambda b: (b, 0)),
        tok_spec,
        pl.BlockSpec((1, HG_W), lambda b: (0, 0)),
    ]
    args = [qdf, kdf, kef, qdb, kdb, keb, v, vt, dec, sg, norm_g]
    st_block = (1, 2, HG_HEADS, HG_DK, HG_DK)
    if has_init:
        in_specs.append(pl.BlockSpec(st_block, lambda b: (b, 0, 0, 0, 0)))
        args.append(s0)
    out_specs = [tok_spec]
    out_shape = [jax.ShapeDtypeStruct((n, HG_W), BF16)]
    if emit_state:
        out_specs.append(pl.BlockSpec(st_block, lambda b: (b, 0, 0, 0, 0)))
        out_shape.append(jax.ShapeDtypeStruct((nb, 2, HG_HEADS, HG_DK, HG_DK), F32))
    res = pl.pallas_call(
        functools.partial(_hgrn_kernel, seq=seq, has_init=has_init, emit_state=emit_state),
        grid=(nb,),
        in_specs=in_specs,
        out_specs=out_specs,
        out_shape=out_shape,
        scratch_shapes=[pltpu.VMEM((seq, HG_W), F32)],
        compiler_params=_cparams(1, 16 * seq * HG_W * 4 + (8 << 20)),
        name="hgrn",
    )(*args)
    return res if emit_state else (res[0], None)


def _shift_rows(x, off, nrows, row):
    if off == 0:
        return x
    y = pltpu.roll(x, (-off) % nrows, axis=0)
    ok = (row + off >= 0) & (row + off < nrows)
    return jnp.where(ok, y, 0.0)


def _scan_rows(a, u, reverse):
    n = a.shape[0]
    row = lax.broadcasted_iota(jnp.int32, a.shape, 0)
    s = 1
    while s < n:
        if reverse:
            a_sh = pltpu.roll(a, n - s, axis=0)
            u_sh = pltpu.roll(u, n - s, axis=0)
            ok = row < n - s
        else:
            a_sh = pltpu.roll(a, s, axis=0)
            u_sh = pltpu.roll(u, s, axis=0)
            ok = row >= s
        u = jnp.where(ok, a * u_sh + u, u)
        a = jnp.where(ok, a * a_sh, a)
        s *= 2
    return a, u


def _rglru_kernel(*refs, seq, grid_rows, has_init, emit_state):
    xr_ref, ggr_ref, cw_ref, cb_ref, wg_ref, bg_ref, lam_ref = refs[:7]
    pos = 7
    h0_ref = None
    if has_init:
        h0_ref = refs[pos]
        pos += 1
    y_ref = refs[pos]
    pos += 1
    st_ref = None
    if emit_state:
        st_ref = refs[pos]
        pos += 1
    a_ref, u_ref, hsum_ref = refs[pos:pos + 3]

    stride = GRID_W if grid_rows else 1
    row = lax.broadcasted_iota(jnp.int32, (seq, RG_W), 0)
    x = xr_ref[...]
    xc = jnp.zeros_like(x) + cb_ref[...]
    for j in range(CONV_W):
        xc = xc + cw_ref[j:j + 1, :] * _shift_rows(x, (j - CONV_LEFT) * stride, seq, row)
    gates = jax.nn.sigmoid(_dot(xc.astype(BF16), wg_ref[...]) + bg_ref[...])

    lam = lam_ref[...]
    nl = -lam
    softplus = jnp.maximum(nl, 0.0) + jnp.log1p(jnp.exp(-jnp.abs(nl)))

    for d in range(2):
        r = gates[:, (2 * d) * RG_W:(2 * d + 1) * RG_W]
        i = gates[:, (2 * d + 1) * RG_W:(2 * d + 2) * RG_W]
        log_a = -RG_C * r * softplus[d:d + 1, :]
        a = jnp.exp(log_a)
        u = jnp.sqrt(jnp.tanh(-log_a) * (a * a + 1.0)) * (i * xc)
        if has_init:
            h0 = h0_ref[0, d:d + 1, :]
        else:
            h0 = jnp.zeros((1, RG_W), F32)
        rev = d == 1
        if not grid_rows:
            acum, hz = _scan_rows(a, u, rev)
            h = hz + acum * h0
            last = h[0:1, :] if rev else h[seq - 1:seq, :]
        else:
            nrow = seq // GRID_W
            a_ref[...] = a
            u_ref[...] = u
            order = range(nrow - 1, -1, -1) if rev else range(nrow)
            hl = jnp.zeros((GRID_W, RG_W), F32)
            ac = jnp.ones((GRID_W, RG_W), F32)
            for rr in order:
                sl = slice(rr * GRID_W, (rr + 1) * GRID_W)
                ar = a_ref[sl, :]
                hl = ar * hl + u_ref[sl, :]
                ac = ar * ac
                u_ref[sl, :] = hl
                a_ref[sl, :] = ac
            ccum, hend0 = _scan_rows(ac, hl, rev)
            hend = hend0 + ccum * h0
            crow = lax.broadcasted_iota(jnp.int32, (GRID_W, RG_W), 0)
            if rev:
                hin = jnp.where(crow == GRID_W - 1, h0, pltpu.roll(hend, GRID_W - 1, axis=0))
                last = hend[0:1, :]
            else:
                hin = jnp.where(crow == 0, h0, pltpu.roll(hend, 1, axis=0))
                last = hend[GRID_W - 1:GRID_W, :]
            for rr in range(nrow):
                sl = slice(rr * GRID_W, (rr + 1) * GRID_W)
                u_ref[sl, :] = u_ref[sl, :] + a_ref[sl, :] * hin
            h = u_ref[...]
        if d == 0:
            hsum_ref[...] = h
        else:
            hsum_ref[...] = hsum_ref[...] + h
        if emit_state:
            st_ref[0, d:d + 1, :] = last
    y_ref[...] = (hsum_ref[...] * ggr_ref[...]).astype(BF16)


def _rglru_call(xr, ggr, conv_w, conv_b, wg, bg, lam, h0, seq, grid_rows, emit_state):
    n = xr.shape[0]
    nb = n // seq
    has_init = h0 is not None
    tok_spec = pl.BlockSpec((seq, RG_W), lambda b: (b, 0))
    full = lambda a: pl.BlockSpec(a.shape, lambda b: (0,) * a.ndim)
    in_specs = [tok_spec, tok_spec, full(conv_w), full(conv_b), full(wg), full(bg), full(lam)]
    args = [xr, ggr, conv_w, conv_b, wg, bg, lam]
    if has_init:
        in_specs.append(pl.BlockSpec((1, 2, RG_W), lambda b: (b, 0, 0)))
        args.append(h0)
    out_specs = [tok_spec]
    out_shape = [jax.ShapeDtypeStruct((n, RG_W), BF16)]
    if emit_state:
        out_specs.append(pl.BlockSpec((1, 2, RG_W), lambda b: (b, 0, 0)))
        out_shape.append(jax.ShapeDtypeStruct((nb, 2, RG_W), F32))
    res = pl.pallas_call(
        functools.partial(_rglru_kernel, seq=seq, grid_rows=grid_rows, has_init=has_init,
                          emit_state=emit_state),
        grid=(nb,),
        in_specs=in_specs,
        out_specs=out_specs,
        out_shape=out_shape,
        scratch_shapes=[pltpu.VMEM((seq, RG_W), F32)] * 3,
        compiler_params=_cparams(1, 20 * seq * RG_W * 4 + (8 << 20)),
        name="rglru",
    )(*args)
    return res if emit_state else (res[0], None)


def _oproj_kernel(og_ref, yrg_ref, x_ref, mod_ref, wo_ref, g_ref, b_ref, x1_ref, h2t_ref):
    mod = mod_ref[0]
    g1 = mod[:, 2 * D_MODEL:3 * D_MODEL]
    sh2 = mod[:, 3 * D_MODEL:4 * D_MODEL]
    sc2 = mod[:, 4 * D_MODEL:5 * D_MODEL]
    mix = _dot(og_ref[...], wo_ref[0:HG_W, :]) + _dot(yrg_ref[...], wo_ref[HG_W:, :])
    x1 = _layer_norm(ALPHA * x_ref[...] + g1 * mix, g_ref[...], b_ref[...])
    x1_ref[...] = x1
    h2 = x1 * (1.0 + sc2) + sh2
    h2t_ref[...] = h2.T.astype(BF16)


def _oproj_call(og, yrg, x2d, mod3, w_out_bf, ln_g, ln_b, mod_row_fn):
    n = x2d.shape[0]
    nt = n // TOK_TILE
    return pl.pallas_call(
        _oproj_kernel,
        grid=(nt,),
        in_specs=[
            pl.BlockSpec((TOK_TILE, HG_W), lambda t: (t, 0)),
            pl.BlockSpec((TOK_TILE, RG_W), lambda t: (t, 0)),
            pl.BlockSpec((TOK_TILE, D_MODEL), lambda t: (t, 0)),
            pl.BlockSpec((1, 1, 6 * D_MODEL), lambda t: (mod_row_fn(t), 0, 0)),
            pl.BlockSpec((D_MODEL, D_MODEL), lambda t: (0, 0)),
            pl.BlockSpec((1, D_MODEL), lambda t: (0, 0)),
            pl.BlockSpec((1, D_MODEL), lambda t: (0, 0)),
        ],
        out_specs=[
            pl.BlockSpec((TOK_TILE, D_MODEL), lambda t: (t, 0)),
            pl.BlockSpec((D_MODEL, TOK_TILE), lambda t: (0, t)),
        ],
        out_shape=[
            jax.ShapeDtypeStruct((n, D_MODEL), F32),
            jax.ShapeDtypeStruct((D_MODEL, n), BF16),
        ],
        compiler_params=_cparams(1, 32 << 20),
        name="oproj",
    )(og, yrg, x2d, mod3, w_out_bf, ln_g, ln_b)


def _cmpx(vals, i, j):
    hi = jnp.maximum(vals[i], vals[j])
    lo = jnp.minimum(vals[i], vals[j])
    vals[i] = hi
    vals[j] = lo


def _bitonic_merge_desc(vals):
    n = len(vals)
    d = n // 2
    while d >= 1:
        for i in range(n):
            if (i & d) == 0:
                _cmpx(vals, i, i + d)
        d //= 2
    return vals


def _sort_desc(vals):
    n = len(vals)
    if n == 1:
        return vals
    top = _sort_desc(vals[:n // 2])
    bot = _sort_desc(vals[n // 2:])
    return _bitonic_merge_desc(top + bot[::-1])


def _merge_top(a, b):
    n = len(a)
    return _bitonic_merge_desc([jnp.maximum(a[i], b[n - 1 - i]) for i in range(n)])


def _merge_sublanes(vals):
    s = SUBLANES // 2
    while s >= 1:
        rolled = [pltpu.roll(v, s, axis=0) for v in vals]
        vals = _merge_top(vals, rolled)
        s //= 2
    return vals


def _top_values(s):
    nv = PEER_NKEYS // SUBLANES
    vals = [s[i * SUBLANES:(i + 1) * SUBLANES, :] for i in range(nv)]
    vals = _sort_desc(vals)
    return _merge_sublanes(vals)


def _route_kernel(ht_ref, wqt_ref, keys_ref, nb_ref, c0_ref, r1_ref, e1_ref):
    tm = ht_ref.shape[1]
    qt = _dot(wqt_ref[...], ht_ref[...]).astype(BF16)
    half = PEER_DQ // 2
    sub = lax.broadcasted_iota(jnp.int32, (SUBLANES, tm), 0)
    for h in range(PEER_HEADS):
        s = []
        top = []
        for p in range(2):
            r0 = (2 * h + p) * half
            sp = _dot(keys_ref[2 * h + p], qt[r0:r0 + half, :])
            s.append(sp)
            top.append(_top_values(sp))
        lists = []
        for gb in range(PEER_TOPK // SUBLANES):
            bsel = jnp.zeros((SUBLANES, tm), F32)
            for b in range(SUBLANES):
                bsel = jnp.where(sub == b, top[1][gb * SUBLANES + b], bsel)
            lists.append([top[0][a] + bsel for a in range(PEER_TOPK)])
        cand = lists[0]
        for gb in range(1, len(lists)):
            cand = _merge_top(cand, lists[gb])
        fv = _merge_sublanes(cand)
        mx = fv[0][0:1, :]
        zsum = jnp.zeros((1, tm), F32)
        for kk in range(PEER_TOPK):
            zsum = zsum + jnp.exp(fv[kk][0:1, :] - mx)
        tau = fv[PEER_TOPK - 1][0:1, :]
        nb = jnp.zeros((PEER_NKEYS, tm), F32)
        rank1 = jnp.zeros((PEER_NKEYS, tm), F32)
        for b in range(PEER_TOPK):
            svb = top[1][b][0:1, :]
            nb = nb + jnp.where(s[0] + svb >= tau, 1.0, 0.0)
            rank1 = rank1 + jnp.where(svb > s[1], 1.0, 0.0)
        nb_ref[h] = nb
        r1_ref[h] = rank1.astype(BF16)
        e1_ref[h] = jnp.exp(s[1] - top[1][0][0:1, :]).astype(BF16)
        c0_ref[h] = jnp.exp(s[0] - top[0][0][0:1, :]) / zsum


def _route_call(h2t, wqt_bf, keys_bf):
    n = h2t.shape[1]
    nt = n // ROUTE_TM
    big = lambda dt: jax.ShapeDtypeStruct((PEER_HEADS, PEER_NKEYS, n), dt)
    big_spec = pl.BlockSpec((PEER_HEADS, PEER_NKEYS, ROUTE_TM), lambda t: (0, 0, t))
    return pl.pallas_call(
        _route_kernel,
        grid=(nt,),
        in_specs=[
            pl.BlockSpec((D_MODEL, ROUTE_TM), lambda t: (0, t)),
            pl.BlockSpec(wqt_bf.shape, lambda t: (0, 0)),
            pl.BlockSpec(keys_bf.shape, lambda t: (0, 0, 0)),
        ],
        out_specs=[big_spec] * 4,
        out_shape=[big(F32), big(F32), big(BF16), big(BF16)],
        compiler_params=_cparams(1, 40 << 20),
        name="route",
    )(h2t, wqt_bf, keys_bf)


def _peer_kernel(ht_ref, u_ref, vt_ref, nb_ref, c0_ref, r1_ref, e1_ref,
                 x1_ref, mod_ref, g_ref, b_ref, out_ref, pt_ref, wt_ref, acc_ref, r1s_ref, e1s_ref):
    step = pl.program_id(1)
    nsteps = pl.num_programs(1)
    tm = ht_ref.shape[1]
    nloc = PEER_EB // PEER_NKEYS

    @pl.when(step == 0)
    def _():
        acc_ref[...] = jnp.zeros_like(acc_ref)
        r1s_ref[...] = r1_ref[...]
        e1s_ref[...] = e1_ref[...]

    pt_ref[...] = _dot(u_ref[...], ht_ref[...]).astype(BF16)

    ig0 = pl.multiple_of(step * nloc, nloc)
    zero = jnp.zeros((PEER_NKEYS, LANES), BF16)
    for mt in range(tm // LANES):
        ls = slice(mt * LANES, (mt + 1) * LANES)
        for il in range(nloc):
            rs = slice(il * PEER_NKEYS, (il + 1) * PEER_NKEYS)
            gate = zero
            for h in range(PEER_HEADS):
                nbrow = nb_ref[h, pl.ds(ig0, nloc), ls][il:il + 1, :].astype(BF16)
                c0row = c0_ref[h, pl.ds(ig0, nloc), ls][il:il + 1, :].astype(BF16)
                sel = jnp.where(r1s_ref[h, :, ls] < nbrow, e1s_ref[h, :, ls], zero)
                gate = gate + sel * c0row
            act = _gelu(pt_ref[rs, ls])
            wt_ref[rs, ls] = gate * act
    acc_ref[...] += _dot(vt_ref[0], wt_ref[...])

    @pl.when(step == nsteps - 1)
    def _():
        mod = mod_ref[0]
        g2 = mod[:, 5 * D_MODEL:6 * D_MODEL]
        ff = acc_ref[...].T
        out_ref[...] = _layer_norm(ALPHA * x1_ref[...] + g2 * ff, g_ref[...], b_ref[...])


def _peer_call(h2t, u_bf, vt_bf, route, x1, mod3, ln_g, ln_b, mod_row_fn):
    nb, c0, r1, e1 = route
    n = h2t.shape[1]
    nt = n // PEER_TM
    nblk = u_bf.shape[0] // PEER_EB
    big_spec = pl.BlockSpec((PEER_HEADS, PEER_NKEYS, PEER_TM), lambda t, i: (0, 0, t))
    return pl.pallas_call(
        _peer_kernel,
        grid=(nt, nblk),
        in_specs=[
            pl.BlockSpec((D_MODEL, PEER_TM), lambda t, i: (0, t)),
            pl.BlockSpec((PEER_EB, D_MODEL), lambda t, i: (i, 0)),
            pl.BlockSpec((1, D_MODEL, PEER_EB), lambda t, i: (i, 0, 0)),
            big_spec, big_spec, big_spec, big_spec,
            pl.BlockSpec((PEER_TM, D_MODEL), lambda t, i: (t, 0)),
            pl.BlockSpec((1, 1, 6 * D_MODEL), lambda t, i: (mod_row_fn(t), 0, 0)),
            pl.BlockSpec((1, D_MODEL), lambda t, i: (0, 0)),
            pl.BlockSpec((1, D_MODEL), lambda t, i: (0, 0)),
        ],
        out_specs=pl.BlockSpec((PEER_TM, D_MODEL), lambda t, i: (t, 0)),
        out_shape=jax.ShapeDtypeStruct((n, D_MODEL), F32),
        scratch_shapes=[
            pltpu.VMEM((PEER_EB, PEER_TM), BF16),
            pltpu.VMEM((PEER_EB, PEER_TM), BF16),
            pltpu.VMEM((D_MODEL, PEER_TM), F32),
            pltpu.VMEM((PEER_HEADS, PEER_NKEYS, PEER_TM), BF16),
            pltpu.VMEM((PEER_HEADS, PEER_NKEYS, PEER_TM), BF16),
        ],
        compiler_params=_cparams(2, 52 << 20),
        name="peer",
    )(h2t, u_bf, vt_bf, nb, c0, r1, e1, x1, mod3, ln_g, ln_b)


def _block_diag(w):
    nh, bw, _ = w.shape
    eye = jnp.eye(nh, dtype=w.dtype)
    return (eye[:, None, :, None] * w[:, :, None, :]).reshape(nh * bw, nh * bw)


def _run_path(x, mod3, weights, s_hg0, s_rg0, seq, grid_rows, emit_state, mod_tok_row):
    bsz = x.shape[0]
    x2d = x.reshape(bsz * seq, D_MODEL)
    row256 = lambda t: mod_tok_row(t, TOK_TILE)
    prep = _inproj_call(x2d, mod3, weights["w_in"], weights["lb"], row256)
    qdf, kdf, kef, qdb, kdb, keb, v, vt, dec, sg, xr, ggr = prep
    og, st_hg = _hgrn_call((qdf, kdf, kef, qdb, kdb, keb, v, vt, dec, sg), weights["hg_norm"],
                           s_hg0, seq, emit_state)
    yrg, st_rg = _rglru_call(xr, ggr, weights["conv_w"], weights["conv_b"], weights["wg"],
                             weights["bg"], weights["lam"], s_rg0, seq, grid_rows, emit_state)
    x1, h2t = _oproj_call(og, yrg, x2d, mod3, weights["w_out"], weights["ln1_g"], weights["ln1_b"],
                          row256)
    route = _route_call(h2t, weights["wqt"], weights["keys"])
    out = _peer_call(h2t, weights["u"], weights["vt"], route, x1, mod3, weights["ln2_g"],
                     weights["ln2_b"], lambda t: mod_tok_row(t, PEER_TM))
    return out.reshape(bsz, seq, D_MODEL), st_hg, st_rg


def kernel(x_prompt, x_sample, c, state_hgrn, state_rglru, c_ctx, w_ada, b_ada, w_in, hgrn_lb,
           hgrn_norm_g, conv_w, conv_b, rg_wr, rg_br, rg_wi, rg_bi, rg_lam, w_out, ln1_g, ln1_b,
           peer_wq, peer_keys, peer_u, peer_v, ln2_g, ln2_b):
    assert w_ada.shape[0] == DEPTH
    bp, seq_p, _ = x_prompt.shape
    bs, seq_s, _ = x_sample.shape
    l = 0
    nrows = -(-(1 + bs) // SUBLANES) * SUBLANES
    cond = jnp.zeros((nrows, D_MODEL), F32).at[0].set(c_ctx).at[1:1 + bs].set(c)
    mod = _mod_call(cond, w_ada[l], b_ada[l][None, :])
    mod3 = mod[:, None, :]

    wg = jnp.concatenate([_block_diag(rg_wr[l, 0]), _block_diag(rg_wi[l, 0]),
                          _block_diag(rg_wr[l, 1]), _block_diag(rg_wi[l, 1])], axis=1)
    bg = jnp.concatenate([rg_br[l, 0], rg_bi[l, 0], rg_br[l, 1], rg_bi[l, 1]])[None, :]
    weights = {
        "w_in": w_in[l].astype(BF16),
        "lb": hgrn_lb,
        "hg_norm": hgrn_norm_g[l][None, :],
        "conv_w": conv_w[l],
        "conv_b": conv_b[l][None, :],
        "wg": wg.astype(BF16),
        "bg": bg,
        "lam": rg_lam[l],
        "w_out": w_out[l].astype(BF16),
        "ln1_g": ln1_g[l][None, :],
        "ln1_b": ln1_b[l][None, :],
        "wqt": peer_wq[l].T.astype(BF16),
        "keys": peer_keys[l].reshape(PEER_HEADS * 2, PEER_NKEYS, PEER_DQ // 2).astype(BF16),
        "u": peer_u[l].astype(BF16),
        "vt": peer_v[l].reshape(-1, PEER_EB, D_MODEL).transpose(0, 2, 1).astype(BF16),
        "ln2_g": ln2_g[l][None, :],
        "ln2_b": ln2_b[l][None, :],
    }

    yp, st_hg, st_rg = _run_path(x_prompt, mod3, weights, None, None, seq_p, False, True,
                                 lambda t, tile: 0)
    ys, _, _ = _run_path(x_sample, mod3, weights, state_hgrn[:, l], state_rglru[:, l], seq_s, True,
                         False, lambda t, tile: 1 + (t * tile) // seq_s)
    new_hg = st_hg[:, None].astype(x_prompt.dtype)
    new_rg = st_rg[:, None].astype(x_prompt.dtype)
    return (yp, ys, new_hg, new_rg)
```

```python
import functools

import jax
import jax.numpy as jnp
from jax import lax
from jax.experimental import pallas as pl
from jax.experimental.pallas import tpu as pltpu

F32 = jnp.float32
BF16 = jnp.bfloat16

D_MODEL = 1024
HG_HEADS = 4
HG_DK = 128
HG_W = 512
RG_W = 512
RG_HEADS = 8
RG_BW = 64
RG_C = 8.0
CONV_W = 4
CONV_LEFT = CONV_W // 2
GRID_W = 64
CHUNK = 32
D_IN = 5 * HG_W + 2 * RG_W
PEER_HEADS = 8
PEER_NKEYS = 128
PEER_TOPK = 16
PEER_DQ = 256
DEPTH = 1
ALPHA = (2.0 * DEPTH) ** 0.25
LN_EPS = 1e-5
RMS_EPS = 1e-6

LANES = 128
SUBLANES = 8
VMEM_CAP_BYTES = 56 * 1024 * 1024

TOK_TILE = 256
GROUP = 128
GPS = SUBLANES * CHUNK // GROUP
PEER_TM = 512
PEER_EB = 1024
PEER_IG = 2
ROUTE_TM = 256


def _cparams(n_axes, vmem_bytes, flags=None):
    return pltpu.CompilerParams(
        dimension_semantics=("arbitrary",) * n_axes,
        vmem_limit_bytes=min(int(vmem_bytes), VMEM_CAP_BYTES),
        flags=flags,
    )


def _silu(x):
    return x * jax.nn.sigmoid(x)


def _gelu(x):
    return jax.nn.gelu(x, approximate=True)


def _dot(a, b):
    return jnp.dot(a, b, preferred_element_type=F32)


def _dot_nt(a, b):
    return lax.dot_general(a, b, (((1,), (1,)), ((), ())), preferred_element_type=F32)


def _layer_norm(x, g, b):
    mu = jnp.mean(x, -1, keepdims=True)
    xc = x - mu
    var = jnp.mean(xc * xc, -1, keepdims=True)
    return xc * lax.rsqrt(var + LN_EPS) * g + b


def _mod_kernel(cond_ref, w_ref, b_ref, o_ref):
    c = _silu(cond_ref[...]).astype(BF16)
    o_ref[...] = _dot(c, w_ref[...].astype(BF16)) + b_ref[...]


def _mod_call(cond, w_ada, b_ada):
    rows = cond.shape[0]
    nblk = w_ada.shape[1] // D_MODEL
    return pl.pallas_call(
        _mod_kernel,
        grid=(nblk,),
        in_specs=[
            pl.BlockSpec((rows, D_MODEL), lambda j: (0, 0)),
            pl.BlockSpec((D_MODEL, D_MODEL), lambda j: (0, j)),
            pl.BlockSpec((1, D_MODEL), lambda j: (0, j)),
        ],
        out_specs=pl.BlockSpec((rows, D_MODEL), lambda j: (0, j)),
        out_shape=jax.ShapeDtypeStruct((rows, w_ada.shape[1]), F32),
        compiler_params=_cparams(1, 24 << 20),
        name="mod",
    )(cond, w_ada, b_ada)


def _mask01(cond):
    return jnp.where(cond, 1.0, 0.0).astype(BF16)


def _split2(x):
    hi = x.astype(BF16)
    lo = (x - hi.astype(F32)).astype(BF16)
    return hi, lo


def _inproj_kernel(x_ref, mod_ref, w_ref, lb_ref,
                   qdf_ref, kdf_ref, kef_ref, qdb_ref, kdb_ref, keb_ref,
                   v_ref, vt_ref, dec_ref, sg_ref, xr_ref, ggr_ref):
    tt = x_ref.shape[0]
    mod = mod_ref[0]
    sh1 = mod[:, 0:D_MODEL]
    sc1 = mod[:, D_MODEL:2 * D_MODEL]
    h = (x_ref[...] * (1.0 + sc1) + sh1).astype(BF16)
    z = _dot(h, w_ref[...])
    q = z[:, 0:HG_W]
    iv = z[:, HG_W:2 * HG_W]
    g = z[:, 4 * HG_W:5 * HG_W]
    xr = z[:, 5 * HG_W:5 * HG_W + RG_W]
    gr = z[:, 5 * HG_W + RG_W:]
    qs = _silu(q)
    ivb = iv.astype(BF16)
    v_ref[...] = ivb
    ivt = iv.T.astype(BF16)
    for gi in range(tt // GROUP):
        vt_ref[gi] = ivt[:, gi * GROUP:(gi + 1) * GROUP]
    sg_ref[...] = _silu(g)
    xr_ref[...] = xr
    ggr_ref[...] = _gelu(gr)

    lbp = lb_ref[...]
    mx = jnp.max(lbp, axis=1, keepdims=True)
    e = jnp.exp(lbp - mx)
    lb_all = e[:, 0, :] / jnp.sum(e, axis=1)

    row = lax.broadcasted_iota(jnp.int32, (tt, tt), 0)
    col = lax.broadcasted_iota(jnp.int32, (tt, tt), 1)
    same = (row // CHUNK) == (col // CHUNK)
    tot = _mask01(same)
    srow = lax.broadcasted_iota(jnp.int32, (tt // CHUNK, tt), 0)
    scol = lax.broadcasted_iota(jnp.int32, (tt // CHUNK, tt), 1)
    sel = _mask01(srow == scol // CHUNK)

    outs = ((qdf_ref, kdf_ref, kef_ref), (qdb_ref, kdb_ref, keb_ref))
    for d in range(2):
        zf = z[:, (2 + d) * HG_W:(3 + d) * HG_W]
        lb = lb_all[d:d + 1, :]
        f = lb + (1.0 - lb) * jax.nn.sigmoid(zf)
        k = (1.0 - lb) * jax.nn.sigmoid(-zf)
        lf = jnp.log(f)
        lf_hi, lf_lo = _split2(lf)
        tri = _mask01(same & ((col <= row) if d == 0 else (col >= row)))
        b = _dot(tri, lf_hi) + _dot(tri, lf_lo)
        btot = _dot(tot, lf_hi) + _dot(tot, lf_lo)
        qd_ref, kd_ref, ke_ref = outs[d]
        qd_ref[...] = (qs * jnp.exp(b)).astype(BF16)
        kd_ref[...] = (k * jnp.exp(-b)).astype(BF16)
        ke_ref[...] = (k * jnp.exp(btot - b)).astype(BF16)
        dec_ref[:, d * HG_W:(d + 1) * HG_W] = jnp.exp(_dot(sel, lf_hi) + _dot(sel, lf_lo))


def _inproj_call(x2d, mod3, w_in_bf, lb_p, mod_row_fn):
    n = x2d.shape[0]
    nt = n // TOK_TILE
    tok_bf = lambda: jax.ShapeDtypeStruct((n, HG_W), BF16)
    tok_f = lambda: jax.ShapeDtypeStruct((n, HG_W), F32)
    tok_spec = pl.BlockSpec((TOK_TILE, HG_W), lambda t: (t, 0))
    gpt = TOK_TILE // GROUP
    cpt = TOK_TILE // CHUNK
    return pl.pallas_call(
        _inproj_kernel,
        grid=(nt,),
        in_specs=[
            pl.BlockSpec((TOK_TILE, D_MODEL), lambda t: (t, 0)),
            pl.BlockSpec((1, 1, 6 * D_MODEL), lambda t: (mod_row_fn(t), 0, 0)),
            pl.BlockSpec((D_MODEL, D_IN), lambda t: (0, 0)),
            pl.BlockSpec(lb_p.shape, lambda t: (0, 0, 0)),
        ],
        out_specs=[tok_spec] * 7 + [
            pl.BlockSpec((gpt, HG_W, GROUP), lambda t: (t, 0, 0)),
            pl.BlockSpec((cpt, 2 * HG_W), lambda t: (t, 0)),
            tok_spec, tok_spec, tok_spec,
        ],
        out_shape=[tok_bf() for _ in range(7)] + [
            jax.ShapeDtypeStruct((n // GROUP, HG_W, GROUP), BF16),
            jax.ShapeDtypeStruct((n // CHUNK, 2 * HG_W), F32),
            tok_f(), tok_f(), tok_f(),
        ],
        compiler_params=_cparams(1, 48 << 20),
        name="inproj",
    )(x2d, mod3, w_in_bf, lb_p)


def _hgrn_kernel(*refs, seq, has_init, emit_state):
    (qdf_ref, kdf_ref, kef_ref, qdb_ref, kdb_ref, keb_ref, v_ref, vt_ref, dec_ref, sg_ref,
     ng_ref) = refs[:11]
    pos = 11
    s0_ref = None
    if has_init:
        s0_ref = refs[pos]
        pos += 1
    og_ref = refs[pos]
    pos += 1
    st_ref = None
    if emit_state:
        st_ref = refs[pos]
        pos += 1
    oacc_ref, stt_ref = refs[pos:pos + 2]

    ngroups = seq // GROUP
    cpg = GROUP // CHUNK
    assert GPS * cpg == SUBLANES and ngroups % GPS == 0
    nsteps = ngroups // GPS
    row = lax.broadcasted_iota(jnp.int32, (GROUP, GROUP), 0)
    col = lax.broadcasted_iota(jnp.int32, (GROUP, GROUP), 1)
    same = (row // CHUNK) == (col // CHUNK)
    dirs = ((qdf_ref, kdf_ref, kef_ref), (qdb_ref, kdb_ref, keb_ref))
    for d in range(2):
        for h in range(HG_HEADS):
            if has_init:
                stt_ref[d, h] = s0_ref[0, d, h].T
            else:
                stt_ref[d, h] = jnp.zeros((HG_DK, HG_DK), F32)

    def body(pi, carry):
        for d in range(2):
            qd_ref, kd_ref, ke_ref = dirs[d]
            keep = same & ((col <= row) if d == 0 else (col >= row))
            pr = pi if d == 0 else nsteps - 1 - pi
            r8 = pl.multiple_of(pr * SUBLANES, SUBLANES)
            for h in range(HG_HEADS):
                cs = slice(h * HG_DK, (h + 1) * HG_DK)
                dec8 = dec_ref[pl.ds(r8, SUBLANES), d * HG_W + h * HG_DK:d * HG_W + (h + 1) * HG_DK]
                st = stt_ref[d, h]
                for gg in range(GPS):
                    sub = gg if d == 0 else GPS - 1 - gg
                    g = pr * GPS + sub
                    r0 = pl.multiple_of(g * GROUP, GROUP)
                    qd = qd_ref[pl.ds(r0, GROUP), cs]
                    kd = kd_ref[pl.ds(r0, GROUP), cs]
                    ke = ke_ref[pl.ds(r0, GROUP), cs]
                    vg = v_ref[pl.ds(r0, GROUP), cs]
                    vtg = vt_ref[g, cs, :]
                    att = jnp.where(keep, _dot_nt(qd, kd), 0.0).astype(BF16)
                    o = _dot(att, vg)
                    parts = [None] * cpg
                    for cc in range(cpg):
                        c = cc if d == 0 else cpg - 1 - cc
                        qc = qd[c * CHUNK:(c + 1) * CHUNK]
                        parts[c] = _dot_nt(qc, st.astype(BF16))
                        kem = ke * _mask01((row // CHUNK) == c)
                        dst = _dot(vtg, kem)
                        decay = dec8[sub * cpg + c:sub * cpg + c + 1, :]
                        st = decay * st + dst
                    oacc_ref[d, pl.ds(r0, GROUP), cs] = o + jnp.concatenate(parts, axis=0)
                stt_ref[d, h] = st
        return carry

    lax.fori_loop(0, nsteps, body, 0)
    if emit_state:
        for d in range(2):
            for h in range(HG_HEADS):
                st_ref[0, d, h] = stt_ref[d, h].T

    for h in range(HG_HEADS):
        cs = slice(h * HG_DK, (h + 1) * HG_DK)
        o = oacc_ref[0, :, cs] + oacc_ref[1, :, cs]
        ms = jnp.mean(o * o, -1, keepdims=True)
        on = o * lax.rsqrt(ms + RMS_EPS) * ng_ref[:, cs]
        og_ref[:, cs] = (on * sg_ref[:, cs]).astype(BF16)


def _hgrn_call(prep, norm_g, s0, seq, emit_state):
    qdf, kdf, kef, qdb, kdb, keb, v, vt, dec, sg = prep
    n = v.shape[0]
    nb = n // seq
    has_init = s0 is not None
    tok_spec = pl.BlockSpec((seq, HG_W), lambda b: (b, 0))
    in_specs = [tok_spec] * 7 + [
        pl.BlockSpec((seq // GROUP, HG_W, GROUP), lambda b: (b, 0, 0)),
        pl.BlockSpec((seq // CHUNK, 2 * HG_W), lambda b: (b, 0)),
        tok_spec,
        pl.BlockSpec((1, HG_W), lambda b: (0, 0)),
    ]
    args = [qdf, kdf, kef, qdb, kdb, keb, v, vt, dec, sg, norm_g]
    st_block = (1, 2, HG_HEADS, HG_DK, HG_DK)
    if has_init:
        in_specs.append(pl.BlockSpec(st_block, lambda b: (b, 0, 0, 0, 0)))
        args.append(s0)
    out_specs = [tok_spec]
    out_shape = [jax.ShapeDtypeStruct((n, HG_W), BF16)]
    if emit_state:
        out_specs.append(pl.BlockSpec(st_block, lambda b: (b, 0, 0, 0, 0)))
        out_shape.append(jax.ShapeDtypeStruct((nb, 2, HG_HEADS, HG_DK, HG_DK), F32))
    res = pl.pallas_call(
        functools.partial(_hgrn_kernel, seq=seq, has_init=has_init, emit_state=emit_state),
        grid=(nb,),
        in_specs=in_specs,
        out_specs=out_specs,
        out_shape=out_shape,
        scratch_shapes=[pltpu.VMEM((2, seq, HG_W), F32),
                        pltpu.VMEM((2, HG_HEADS, HG_DK, HG_DK), F32)],
        compiler_params=_cparams(1, 16 * seq * HG_W * 4 + (8 << 20)),
        name="hgrn",
    )(*args)
    return res if emit_state else (res[0], None)


def _shift_rows(x, off, nrows, row):
    if off == 0:
        return x
    y = pltpu.roll(x, (-off) % nrows, axis=0)
    ok = (row + off >= 0) & (row + off < nrows)
    return jnp.where(ok, y, 0.0)


def _scan_rows(a, u, reverse):
    n = a.shape[0]
    row = lax.broadcasted_iota(jnp.int32, a.shape, 0)
    s = 1
    while s < n:
        if reverse:
            a_sh = pltpu.roll(a, n - s, axis=0)
            u_sh = pltpu.roll(u, n - s, axis=0)
            ok = row < n - s
        else:
            a_sh = pltpu.roll(a, s, axis=0)
            u_sh = pltpu.roll(u, s, axis=0)
            ok = row >= s
        u = jnp.where(ok, a * u_sh + u, u)
        a = jnp.where(ok, a * a_sh, a)
        s *= 2
    return a, u


def _rglru_kernel(*refs, seq, grid_rows, has_init, emit_state):
    xr_ref, ggr_ref, cw_ref, cb_ref, wg_ref, bg_ref, lam_ref = refs[:7]
    pos = 7
    h0_ref = None
    if has_init:
        h0_ref = refs[pos]
        pos += 1
    y_ref = refs[pos]
    pos += 1
    st_ref = None
    if emit_state:
        st_ref = refs[pos]
        pos += 1
    a_ref, u_ref, hsum_ref = refs[pos:pos + 3]

    stride = GRID_W if grid_rows else 1
    row = lax.broadcasted_iota(jnp.int32, (seq, RG_W), 0)
    x = xr_ref[...]
    xc = jnp.zeros_like(x) + cb_ref[...]
    for j in range(CONV_W):
        xc = xc + cw_ref[j:j + 1, :] * _shift_rows(x, (j - CONV_LEFT) * stride, seq, row)
    gates = jax.nn.sigmoid(_dot(xc.astype(BF16), wg_ref[...]) + bg_ref[...])

    lam = lam_ref[...]
    nl = -lam
    softplus = jnp.maximum(nl, 0.0) + jnp.log1p(jnp.exp(-jnp.abs(nl)))

    for d in range(2):
        r = gates[:, (2 * d) * RG_W:(2 * d + 1) * RG_W]
        i = gates[:, (2 * d + 1) * RG_W:(2 * d + 2) * RG_W]
        log_a = -RG_C * r * softplus[d:d + 1, :]
        a = jnp.exp(log_a)
        u = jnp.sqrt(jnp.tanh(-log_a) * (a * a + 1.0)) * (i * xc)
        if has_init:
            h0 = h0_ref[0, d:d + 1, :]
        else:
            h0 = jnp.zeros((1, RG_W), F32)
        rev = d == 1
        if not grid_rows:
            acum, hz = _scan_rows(a, u, rev)
            h = hz + acum * h0
            last = h[0:1, :] if rev else h[seq - 1:seq, :]
        else:
            nrow = seq // GRID_W
            a_ref[...] = a
            u_ref[...] = u
            order = range(nrow - 1, -1, -1) if rev else range(nrow)
            hl = jnp.zeros((GRID_W, RG_W), F32)
            ac = jnp.ones((GRID_W, RG_W), F32)
            for rr in order:
                sl = slice(rr * GRID_W, (rr + 1) * GRID_W)
                ar = a_ref[sl, :]
                hl = ar * hl + u_ref[sl, :]
                ac = ar * ac
                u_ref[sl, :] = hl
                a_ref[sl, :] = ac
            ccum, hend0 = _scan_rows(ac, hl, rev)
            hend = hend0 + ccum * h0
            crow = lax.broadcasted_iota(jnp.int32, (GRID_W, RG_W), 0)
            if rev:
                hin = jnp.where(crow == GRID_W - 1, h0, pltpu.roll(hend, GRID_W - 1, axis=0))
                last = hend[0:1, :]
            else:
                hin = jnp.where(crow == 0, h0, pltpu.roll(hend, 1, axis=0))
                last = hend[GRID_W - 1:GRID_W, :]
            for rr in range(nrow):
                sl = slice(rr * GRID_W, (rr + 1) * GRID_W)
                u_ref[sl, :] = u_ref[sl, :] + a_ref[sl, :] * hin
            h = u_ref[...]
        if d == 0:
            hsum_ref[...] = h
        else:
            hsum_ref[...] = hsum_ref[...] + h
        if emit_state:
            st_ref[0, d:d + 1, :] = last
    y_ref[...] = (hsum_ref[...] * ggr_ref[...]).astype(BF16)


def _rglru_call(xr, ggr, conv_w, conv_b, wg, bg, lam, h0, seq, grid_rows, emit_state):
    n = xr.shape[0]
    nb = n // seq
    has_init = h0 is not None
    tok_spec = pl.BlockSpec((seq, RG_W), lambda b: (b, 0))
    full = lambda a: pl.BlockSpec(a.shape, lambda b: (0,) * a.ndim)
    in_specs = [tok_spec, tok_spec, full(conv_w), full(conv_b), full(wg), full(bg), full(lam)]
    args = [xr, ggr, conv_w, conv_b, wg, bg, lam]
    if has_init:
        in_specs.append(pl.BlockSpec((1, 2, RG_W), lambda b: (b, 0, 0)))
        args.append(h0)
    out_specs = [tok_spec]
    out_shape = [jax.ShapeDtypeStruct((n, RG_W), BF16)]
    if emit_state:
        out_specs.append(pl.BlockSpec((1, 2, RG_W), lambda b: (b, 0, 0)))
        out_shape.append(jax.ShapeDtypeStruct((nb, 2, RG_W), F32))
    res = pl.pallas_call(
        functools.partial(_rglru_kernel, seq=seq, grid_rows=grid_rows, has_init=has_init,
                          emit_state=emit_state),
        grid=(nb,),
        in_specs=in_specs,
        out_specs=out_specs,
        out_shape=out_shape,
        scratch_shapes=[pltpu.VMEM((seq, RG_W), F32)] * 3,
        compiler_params=_cparams(1, 20 * seq * RG_W * 4 + (8 << 20)),
        name="rglru",
    )(*args)
    return res if emit_state else (res[0], None)


def _oproj_kernel(og_ref, yrg_ref, x_ref, mod_ref, wo_ref, g_ref, b_ref, x1_ref, h2t_ref):
    mod = mod_ref[0]
    g1 = mod[:, 2 * D_MODEL:3 * D_MODEL]
    sh2 = mod[:, 3 * D_MODEL:4 * D_MODEL]
    sc2 = mod[:, 4 * D_MODEL:5 * D_MODEL]
    mix = _dot(og_ref[...], wo_ref[0:HG_W, :]) + _dot(yrg_ref[...], wo_ref[HG_W:, :])
    x1 = _layer_norm(ALPHA * x_ref[...] + g1 * mix, g_ref[...], b_ref[...])
    x1_ref[...] = x1
    h2 = x1 * (1.0 + sc2) + sh2
    h2t_ref[...] = h2.T.astype(BF16)


def _oproj_call(og, yrg, x2d, mod3, w_out_bf, ln_g, ln_b, mod_row_fn):
    n = x2d.shape[0]
    nt = n // TOK_TILE
    return pl.pallas_call(
        _oproj_kernel,
        grid=(nt,),
        in_specs=[
            pl.BlockSpec((TOK_TILE, HG_W), lambda t: (t, 0)),
            pl.BlockSpec((TOK_TILE, RG_W), lambda t: (t, 0)),
            pl.BlockSpec((TOK_TILE, D_MODEL), lambda t: (t, 0)),
            pl.BlockSpec((1, 1, 6 * D_MODEL), lambda t: (mod_row_fn(t), 0, 0)),
            pl.BlockSpec((D_MODEL, D_MODEL), lambda t: (0, 0)),
            pl.BlockSpec((1, D_MODEL), lambda t: (0, 0)),
            pl.BlockSpec((1, D_MODEL), lambda t: (0, 0)),
        ],
        out_specs=[
            pl.BlockSpec((TOK_TILE, D_MODEL), lambda t: (t, 0)),
            pl.BlockSpec((D_MODEL, TOK_TILE), lambda t: (0, t)),
        ],
        out_shape=[
            jax.ShapeDtypeStruct((n, D_MODEL), F32),
            jax.ShapeDtypeStruct((D_MODEL, n), BF16),
        ],
        compiler_params=_cparams(1, 32 << 20),
        name="oproj",
    )(og, yrg, x2d, mod3, w_out_bf, ln_g, ln_b)


def _cmpx(vals, i, j):
    hi = jnp.maximum(vals[i], vals[j])
    lo = jnp.minimum(vals[i], vals[j])
    vals[i] = hi
    vals[j] = lo


def _bitonic_merge_desc(vals):
    n = len(vals)
    d = n // 2
    while d >= 1:
        for i in range(n):
            if (i & d) == 0:
                _cmpx(vals, i, i + d)
        d //= 2
    return vals


def _sort_desc(vals):
    n = len(vals)
    if n == 1:
        return vals
    top = _sort_desc(vals[:n // 2])
    bot = _sort_desc(vals[n // 2:])
    return _bitonic_merge_desc(top + bot[::-1])


def _merge_top(a, b):
    n = len(a)
    return _bitonic_merge_desc([jnp.maximum(a[i], b[n - 1 - i]) for i in range(n)])


def _merge_sublanes(vals):
    s = SUBLANES // 2
    while s >= 1:
        rolled = [pltpu.roll(v, s, axis=0) for v in vals]
        vals = _merge_top(vals, rolled)
        s //= 2
    return vals


def _top_values(s):
    nv = PEER_NKEYS // SUBLANES
    vals = [s[i * SUBLANES:(i + 1) * SUBLANES, :] for i in range(nv)]
    vals = _sort_desc(vals)
    return _merge_sublanes(vals)


def _count_prefix(pred, rows):
    assert len(rows) == 16
    t16 = pred(rows[15])
    t8 = pred(rows[7])
    t4 = pred(jnp.where(t8, rows[11], rows[3]))
    t2 = pred(jnp.where(t8, jnp.where(t4, rows[13], rows[9]), jnp.where(t4, rows[5], rows[1])))
    hi = jnp.where(t4, jnp.where(t2, rows[14], rows[12]), jnp.where(t2, rows[10], rows[8]))
    lo = jnp.where(t4, jnp.where(t2, rows[6], rows[4]), jnp.where(t2, rows[2], rows[0]))
    t1 = pred(jnp.where(t8, hi, lo))
    cnt = (jnp.where(t8, 8.0, 0.0) + jnp.where(t4, 4.0, 0.0)
           + jnp.where(t2, 2.0, 0.0) + jnp.where(t1, 1.0, 0.0))
    return jnp.where(t16, 16.0, cnt)


def _route_kernel(ht_ref, wqt_ref, keys_ref, nb_ref, c0_ref, r1_ref, e1_ref):
    tm = ht_ref.shape[1]
    qt = _dot(wqt_ref[...], ht_ref[...]).astype(BF16)
    half = PEER_DQ // 2
    sub = lax.broadcasted_iota(jnp.int32, (SUBLANES, tm), 0)
    for h in range(PEER_HEADS):
        s = []
        top = []
        for p in range(2):
            r0 = (2 * h + p) * half
            sp = _dot(keys_ref[2 * h + p], qt[r0:r0 + half, :])
            s.append(sp)
            top.append(_top_values(sp))
        lists = []
        for gb in range(PEER_TOPK // SUBLANES):
            bsel = jnp.zeros((SUBLANES, tm), F32)
            for b in range(SUBLANES):
                bsel = jnp.where(sub == b, top[1][gb * SUBLANES + b], bsel)
            lists.append([top[0][a] + bsel for a in range(PEER_TOPK)])
        cand = lists[0]
        for gb in range(1, len(lists)):
            cand = _merge_top(cand, lists[gb])
        fv = _merge_sublanes(cand)
        mx = fv[0][0:1, :]
        zsum = jnp.zeros((1, tm), F32)
        for kk in range(PEER_TOPK):
            zsum = zsum + jnp.exp(fv[kk][0:1, :] - mx)
        tau = fv[PEER_TOPK - 1][0:1, :]
        sv1 = [top[1][b][0:1, :] for b in range(PEER_TOPK)]
        rb = 4 * SUBLANES
        for r0 in range(0, PEER_NKEYS, rb):
            s0b = s[0][r0:r0 + rb, :]
            s1b = s[1][r0:r0 + rb, :]
            nb_ref[h, r0:r0 + rb, :] = _count_prefix(lambda r: s0b + r >= tau, sv1)
            r1_ref[h, r0:r0 + rb, :] = _count_prefix(lambda r: r > s1b, sv1).astype(BF16)
        e1_ref[h] = jnp.exp(s[1] - top[1][0][0:1, :]).astype(BF16)
        c0_ref[h] = jnp.exp(s[0] - top[0][0][0:1, :]) / zsum


def _route_call(h2t, wqt_bf, keys_bf):
    n = h2t.shape[1]
    nt = n // ROUTE_TM
    big = lambda dt: jax.ShapeDtypeStruct((PEER_HEADS, PEER_NKEYS, n), dt)
    big_spec = pl.BlockSpec((PEER_HEADS, PEER_NKEYS, ROUTE_TM), lambda t: (0, 0, t))
    return pl.pallas_call(
        _route_kernel,
        grid=(nt,),
        in_specs=[
            pl.BlockSpec((D_MODEL, ROUTE_TM), lambda t: (0, t)),
            pl.BlockSpec(wqt_bf.shape, lambda t: (0, 0)),
            pl.BlockSpec(keys_bf.shape, lambda t: (0, 0, 0)),
        ],
        out_specs=[big_spec] * 4,
        out_shape=[big(F32), big(F32), big(BF16), big(BF16)],
        compiler_params=_cparams(1, 40 << 20),
        name="route",
    )(h2t, wqt_bf, keys_bf)


def _peer_kernel(ht_ref, u_ref, vt_ref, nb_ref, c0_ref, r1_ref, e1_ref,
                 x1_ref, mod_ref, g_ref, b_ref, out_ref, pt_ref, wt_ref, acc_ref, re_ref):
    step = pl.program_id(1)
    nsteps = pl.num_programs(1)
    tm = ht_ref.shape[1]
    nloc = PEER_EB // PEER_NKEYS

    @pl.when(step == 0)
    def _():
        acc_ref[...] = jnp.zeros_like(acc_ref)
        re_ref[:, 0, :, 0:tm] = r1_ref[...]
        re_ref[:, 1, :, LANES:LANES + tm] = e1_ref[...]

    pt_ref[...] = _dot(u_ref[...], ht_ref[...]).astype(BF16)

    ig0 = pl.multiple_of(step * nloc, nloc)
    zero = jnp.zeros((PEER_NKEYS, LANES), BF16)
    for mt in range(tm // LANES):
        ls = slice(mt * LANES, (mt + 1) * LANES)
        for iq in range(nloc // PEER_IG):
            gates = [zero] * PEER_IG
            for h in range(PEER_HEADS):
                nbt = nb_ref[h, pl.ds(ig0, nloc), ls]
                c0t = c0_ref[h, pl.ds(ig0, nloc), ls]
                r1 = re_ref[h, 0, :, ls]
                e1 = re_ref[h, 1, :, (mt + 1) * LANES:(mt + 2) * LANES]
                for k in range(PEER_IG):
                    il = iq * PEER_IG + k
                    nbrow = nbt[il:il + 1, :].astype(BF16)
                    c0row = c0t[il:il + 1, :].astype(BF16)
                    gates[k] = gates[k] + jnp.where(r1 < nbrow, e1, zero) * c0row
            for k in range(PEER_IG):
                il = iq * PEER_IG + k
                rs = slice(il * PEER_NKEYS, (il + 1) * PEER_NKEYS)
                wt_ref[rs, ls] = gates[k] * _gelu(pt_ref[rs, ls])
    acc_ref[...] += _dot(vt_ref[0], wt_ref[...])

    @pl.when(step == nsteps - 1)
    def _():
        mod = mod_ref[0]
        g2 = mod[:, 5 * D_MODEL:6 * D_MODEL]
        ff = acc_ref[...].T
        out_ref[...] = _layer_norm(ALPHA * x1_ref[...] + g2 * ff, g_ref[...], b_ref[...])


def _peer_call(h2t, u_bf, vt_bf, route, x1, mod3, ln_g, ln_b, mod_row_fn):
    nb, c0, r1, e1 = route
    n = h2t.shape[1]
    nt = n // PEER_TM
    nblk = u_bf.shape[0] // PEER_EB
    big_spec = pl.BlockSpec((PEER_HEADS, PEER_NKEYS, PEER_TM), lambda t, i: (0, 0, t))
    return pl.pallas_call(
        _peer_kernel,
        grid=(nt, nblk),
        in_specs=[
            pl.BlockSpec((D_MODEL, PEER_TM), lambda t, i: (0, t)),
            pl.BlockSpec((PEER_EB, D_MODEL), lambda t, i: (i, 0)),
            pl.BlockSpec((1, D_MODEL, PEER_EB), lambda t, i: (i, 0, 0)),
            big_spec, big_spec, big_spec, big_spec,
            pl.BlockSpec((PEER_TM, D_MODEL), lambda t, i: (t, 0)),
            pl.BlockSpec((1, 1, 6 * D_MODEL), lambda t, i: (mod_row_fn(t), 0, 0)),
            pl.BlockSpec((1, D_MODEL), lambda t, i: (0, 0)),
            pl.BlockSpec((1, D_MODEL), lambda t, i: (0, 0)),
        ],
        out_specs=pl.BlockSpec((PEER_TM, D_MODEL), lambda t, i: (t, 0)),
        out_shape=jax.ShapeDtypeStruct((n, D_MODEL), F32),
        scratch_shapes=[
            pltpu.VMEM((PEER_EB, PEER_TM), BF16),
            pltpu.VMEM((PEER_EB, PEER_TM), BF16),
            pltpu.VMEM((D_MODEL, PEER_TM), F32),
            pltpu.VMEM((PEER_HEADS, 2, PEER_NKEYS, PEER_TM + LANES), BF16),
        ],
        compiler_params=_cparams(2, 52 << 20),
        name="peer",
    )(h2t, u_bf, vt_bf, nb, c0, r1, e1, x1, mod3, ln_g, ln_b)


def _block_diag(w):
    nh, bw, _ = w.shape
    eye = jnp.eye(nh, dtype=w.dtype)
    return (eye[:, None, :, None] * w[:, :, None, :]).reshape(nh * bw, nh * bw)


def _run_path(x, mod3, weights, s_hg0, s_rg0, seq, grid_rows, emit_state, mod_tok_row):
    bsz = x.shape[0]
    x2d = x.reshape(bsz * seq, D_MODEL)
    row256 = lambda t: mod_tok_row(t, TOK_TILE)
    prep = _inproj_call(x2d, mod3, weights["w_in"], weights["lb"], row256)
    qdf, kdf, kef, qdb, kdb, keb, v, vt, dec, sg, xr, ggr = prep
    og, st_hg = _hgrn_call((qdf, kdf, kef, qdb, kdb, keb, v, vt, dec, sg), weights["hg_norm"],
                           s_hg0, seq, emit_state)
    yrg, st_rg = _rglru_call(xr, ggr, weights["conv_w"], weights["conv_b"], weights["wg"],
                             weights["bg"], weights["lam"], s_rg0, seq, grid_rows, emit_state)
    x1, h2t = _oproj_call(og, yrg, x2d, mod3, weights["w_out"], weights["ln1_g"], weights["ln1_b"],
                          row256)
    route = _route_call(h2t, weights["wqt"], weights["keys"])
    out = _peer_call(h2t, weights["u"], weights["vt"], route, x1, mod3, weights["ln2_g"],
                     weights["ln2_b"], lambda t: mod_tok_row(t, PEER_TM))
    return out.reshape(bsz, seq, D_MODEL), st_hg, st_rg


def kernel(x_prompt, x_sample, c, state_hgrn, state_rglru, c_ctx, w_ada, b_ada, w_in, hgrn_lb,
           hgrn_norm_g, conv_w, conv_b, rg_wr, rg_br, rg_wi, rg_bi, rg_lam, w_out, ln1_g, ln1_b,
           peer_wq, peer_keys, peer_u, peer_v, ln2_g, ln2_b):
    assert w_ada.shape[0] == DEPTH
    bp, seq_p, _ = x_prompt.shape
    bs, seq_s, _ = x_sample.shape
    l = 0
    nrows = -(-(1 + bs) // SUBLANES) * SUBLANES
    cond = jnp.zeros((nrows, D_MODEL), F32).at[0].set(c_ctx).at[1:1 + bs].set(c)
    mod = _mod_call(cond, w_ada[l], b_ada[l][None, :])
    mod3 = mod[:, None, :]

    wg = jnp.concatenate([_block_diag(rg_wr[l, 0]), _block_diag(rg_wi[l, 0]),
                          _block_diag(rg_wr[l, 1]), _block_diag(rg_wi[l, 1])], axis=1)
    bg = jnp.concatenate([rg_br[l, 0], rg_bi[l, 0], rg_br[l, 1], rg_bi[l, 1]])[None, :]
    weights = {
        "w_in": w_in[l].astype(BF16),
        "lb": hgrn_lb,
        "hg_norm": hgrn_norm_g[l][None, :],
        "conv_w": conv_w[l],
        "conv_b": conv_b[l][None, :],
        "wg": wg.astype(BF16),
        "bg": bg,
        "lam": rg_lam[l],
        "w_out": w_out[l].astype(BF16),
        "ln1_g": ln1_g[l][None, :],
        "ln1_b": ln1_b[l][None, :],
        "wqt": peer_wq[l].T.astype(BF16),
        "keys": peer_keys[l].reshape(PEER_HEADS * 2, PEER_NKEYS, PEER_DQ // 2).astype(BF16),
        "u": peer_u[l].astype(BF16),
        "vt": peer_v[l].reshape(-1, PEER_EB, D_MODEL).transpose(0, 2, 1).astype(BF16),
        "ln2_g": ln2_g[l][None, :],
        "ln2_b": ln2_b[l][None, :],
    }

    yp, st_hg, st_rg = _run_path(x_prompt, mod3, weights, None, None, seq_p, False, True,
                                 lambda t, tile: 0)
    ys, _, _ = _run_path(x_sample, mod3, weights, state_hgrn[:, l], state_rglru[:, l], seq_s, True,
                         False, lambda t, tile: 1 + (t * tile) // seq_s)
    new_hg = st_hg[:, None].astype(x_prompt.dtype)
    new_rg = st_rg[:, None].astype(x_prompt.dtype)
    return (yp, ys, new_hg, new_rg)
```

```python
import functools

import jax
import jax.numpy as jnp
from jax import lax
from jax.experimental import pallas as pl
from jax.experimental.pallas import tpu as pltpu

F32 = jnp.float32
BF16 = jnp.bfloat16

D_MODEL = 1024
HG_HEADS = 4
HG_DK = 128
HG_W = 512
RG_W = 512
RG_HEADS = 8
RG_BW = 64
RG_C = 8.0
CONV_W = 4
CONV_LEFT = CONV_W // 2
GRID_W = 64
CHUNK = 32
D_IN = 5 * HG_W + 2 * RG_W
PEER_HEADS = 8
PEER_NKEYS = 128
PEER_TOPK = 16
PEER_DQ = 256
DEPTH = 1
ALPHA = (2.0 * DEPTH) ** 0.25
LN_EPS = 1e-5
RMS_EPS = 1e-6

LANES = 128
SUBLANES = 8
VMEM_CAP_BYTES = 56 * 1024 * 1024

TOK_TILE = 256
GROUP = 128
GPS = SUBLANES * CHUNK // GROUP
PEER_TM = 512
PEER_EB = 1024
PEER_IG = 4
ROUTE_TM = 256


def _cparams(n_axes, vmem_bytes, flags=None):
    return pltpu.CompilerParams(
        dimension_semantics=("arbitrary",) * n_axes,
        vmem_limit_bytes=min(int(vmem_bytes), VMEM_CAP_BYTES),
        flags=flags,
    )


def _silu(x):
    return x * jax.nn.sigmoid(x)


def _gelu(x):
    return jax.nn.gelu(x, approximate=True)


def _dot(a, b):
    return jnp.dot(a, b, preferred_element_type=F32)


def _dot_nt(a, b):
    return lax.dot_general(a, b, (((1,), (1,)), ((), ())), preferred_element_type=F32)


def _layer_norm(x, g, b):
    mu = jnp.mean(x, -1, keepdims=True)
    xc = x - mu
    var = jnp.mean(xc * xc, -1, keepdims=True)
    return xc * lax.rsqrt(var + LN_EPS) * g + b


def _mod_kernel(cond_ref, w_ref, b_ref, o_ref):
    c = _silu(cond_ref[...]).astype(BF16)
    o_ref[...] = _dot(c, w_ref[...].astype(BF16)) + b_ref[...]


def _mod_call(cond, w_ada, b_ada):
    rows = cond.shape[0]
    nblk = w_ada.shape[1] // D_MODEL
    return pl.pallas_call(
        _mod_kernel,
        grid=(nblk,),
        in_specs=[
            pl.BlockSpec((rows, D_MODEL), lambda j: (0, 0)),
            pl.BlockSpec((D_MODEL, D_MODEL), lambda j: (0, j)),
            pl.BlockSpec((1, D_MODEL), lambda j: (0, j)),
        ],
        out_specs=pl.BlockSpec((rows, D_MODEL), lambda j: (0, j)),
        out_shape=jax.ShapeDtypeStruct((rows, w_ada.shape[1]), F32),
        compiler_params=_cparams(1, 24 << 20),
        name="mod",
    )(cond, w_ada, b_ada)


def _mask01(cond):
    return jnp.where(cond, 1.0, 0.0).astype(BF16)


def _split2(x):
    hi = x.astype(BF16)
    lo = (x - hi.astype(F32)).astype(BF16)
    return hi, lo


def _inproj_kernel(x_ref, mod_ref, w_ref, lb_ref,
                   qdf_ref, kdf_ref, kef_ref, qdb_ref, kdb_ref, keb_ref,
                   v_ref, vt_ref, dec_ref, sg_ref, xr_ref, ggr_ref):
    tt = x_ref.shape[0]
    mod = mod_ref[0]
    sh1 = mod[:, 0:D_MODEL]
    sc1 = mod[:, D_MODEL:2 * D_MODEL]
    h = (x_ref[...] * (1.0 + sc1) + sh1).astype(BF16)
    z = _dot(h, w_ref[...])
    q = z[:, 0:HG_W]
    iv = z[:, HG_W:2 * HG_W]
    g = z[:, 4 * HG_W:5 * HG_W]
    xr = z[:, 5 * HG_W:5 * HG_W + RG_W]
    gr = z[:, 5 * HG_W + RG_W:]
    qs = _silu(q)
    ivb = iv.astype(BF16)
    v_ref[...] = ivb
    ivt = iv.T.astype(BF16)
    for gi in range(tt // GROUP):
        vt_ref[gi] = ivt[:, gi * GROUP:(gi + 1) * GROUP]
    sg_ref[...] = _silu(g)
    xr_ref[...] = xr
    ggr_ref[...] = _gelu(gr)

    lbp = lb_ref[...]
    mx = jnp.max(lbp, axis=1, keepdims=True)
    e = jnp.exp(lbp - mx)
    lb_all = e[:, 0, :] / jnp.sum(e, axis=1)

    row = lax.broadcasted_iota(jnp.int32, (tt, tt), 0)
    col = lax.broadcasted_iota(jnp.int32, (tt, tt), 1)
    same = (row // CHUNK) == (col // CHUNK)
    tot = _mask01(same)
    srow = lax.broadcasted_iota(jnp.int32, (tt // CHUNK, tt), 0)
    scol = lax.broadcasted_iota(jnp.int32, (tt // CHUNK, tt), 1)
    sel = _mask01(srow == scol // CHUNK)

    outs = ((qdf_ref, kdf_ref, kef_ref), (qdb_ref, kdb_ref, keb_ref))
    for d in range(2):
        zf = z[:, (2 + d) * HG_W:(3 + d) * HG_W]
        lb = lb_all[d:d + 1, :]
        f = lb + (1.0 - lb) * jax.nn.sigmoid(zf)
        k = (1.0 - lb) * jax.nn.sigmoid(-zf)
        lf = jnp.log(f)
        lf_hi, lf_lo = _split2(lf)
        tri = _mask01(same & ((col <= row) if d == 0 else (col >= row)))
        b = _dot(tri, lf_hi) + _dot(tri, lf_lo)
        btot = _dot(tot, lf_hi) + _dot(tot, lf_lo)
        qd_ref, kd_ref, ke_ref = outs[d]
        qd_ref[...] = (qs * jnp.exp(b)).astype(BF16)
        kd_ref[...] = (k * jnp.exp(-b)).astype(BF16)
        ke_ref[...] = (k * jnp.exp(btot - b)).astype(BF16)
        dec_ref[:, d * HG_W:(d + 1) * HG_W] = jnp.exp(_dot(sel, lf_hi) + _dot(sel, lf_lo))


def _inproj_call(x2d, mod3, w_in_bf, lb_p, mod_row_fn):
    n = x2d.shape[0]
    nt = n // TOK_TILE
    tok_bf = lambda: jax.ShapeDtypeStruct((n, HG_W), BF16)
    tok_f = lambda: jax.ShapeDtypeStruct((n, HG_W), F32)
    tok_spec = pl.BlockSpec((TOK_TILE, HG_W), lambda t: (t, 0))
    gpt = TOK_TILE // GROUP
    cpt = TOK_TILE // CHUNK
    return pl.pallas_call(
        _inproj_kernel,
        grid=(nt,),
        in_specs=[
            pl.BlockSpec((TOK_TILE, D_MODEL), lambda t: (t, 0)),
            pl.BlockSpec((1, 1, 6 * D_MODEL), lambda t: (mod_row_fn(t), 0, 0)),
            pl.BlockSpec((D_MODEL, D_IN), lambda t: (0, 0)),
            pl.BlockSpec(lb_p.shape, lambda t: (0, 0, 0)),
        ],
        out_specs=[tok_spec] * 7 + [
            pl.BlockSpec((gpt, HG_W, GROUP), lambda t: (t, 0, 0)),
            pl.BlockSpec((cpt, 2 * HG_W), lambda t: (t, 0)),
            tok_spec, tok_spec, tok_spec,
        ],
        out_shape=[tok_bf() for _ in range(7)] + [
            jax.ShapeDtypeStruct((n // GROUP, HG_W, GROUP), BF16),
            jax.ShapeDtypeStruct((n // CHUNK, 2 * HG_W), F32),
            tok_f(), tok_f(), tok_f(),
        ],
        compiler_params=_cparams(1, 48 << 20),
        name="inproj",
    )(x2d, mod3, w_in_bf, lb_p)


def _hgrn_kernel(*refs, seq, has_init, emit_state):
    (qdf_ref, kdf_ref, kef_ref, qdb_ref, kdb_ref, keb_ref, v_ref, vt_ref, dec_ref, sg_ref,
     ng_ref) = refs[:11]
    pos = 11
    s0_ref = None
    if has_init:
        s0_ref = refs[pos]
        pos += 1
    og_ref = refs[pos]
    pos += 1
    st_ref = None
    if emit_state:
        st_ref = refs[pos]
        pos += 1
    oacc_ref, stt_ref = refs[pos:pos + 2]

    ngroups = seq // GROUP
    cpg = GROUP // CHUNK
    assert GPS * cpg == SUBLANES and ngroups % GPS == 0
    nsteps = ngroups // GPS
    row = lax.broadcasted_iota(jnp.int32, (GROUP, GROUP), 0)
    col = lax.broadcasted_iota(jnp.int32, (GROUP, GROUP), 1)
    same = (row // CHUNK) == (col // CHUNK)
    dirs = ((qdf_ref, kdf_ref, kef_ref), (qdb_ref, kdb_ref, keb_ref))
    for d in range(2):
        for h in range(HG_HEADS):
            if has_init:
                stt_ref[d, h] = s0_ref[0, d, h].T
            else:
                stt_ref[d, h] = jnp.zeros((HG_DK, HG_DK), F32)

    def body(pi, carry):
        for d in range(2):
            qd_ref, kd_ref, ke_ref = dirs[d]
            keep = same & ((col <= row) if d == 0 else (col >= row))
            pr = pi if d == 0 else nsteps - 1 - pi
            r8 = pl.multiple_of(pr * SUBLANES, SUBLANES)
            for h in range(HG_HEADS):
                cs = slice(h * HG_DK, (h + 1) * HG_DK)
                dec8 = dec_ref[pl.ds(r8, SUBLANES), d * HG_W + h * HG_DK:d * HG_W + (h + 1) * HG_DK]
                st = stt_ref[d, h]
                for gg in range(GPS):
                    sub = gg if d == 0 else GPS - 1 - gg
                    g = pr * GPS + sub
                    r0 = pl.multiple_of(g * GROUP, GROUP)
                    qd = qd_ref[pl.ds(r0, GROUP), cs]
                    kd = kd_ref[pl.ds(r0, GROUP), cs]
                    ke = ke_ref[pl.ds(r0, GROUP), cs]
                    vg = v_ref[pl.ds(r0, GROUP), cs]
                    vtg = vt_ref[g, cs, :]
                    att = jnp.where(keep, _dot_nt(qd, kd), 0.0).astype(BF16)
                    o = _dot(att, vg)
                    vstack = jnp.concatenate(
                        [vtg * _mask01((col // CHUNK) == c) for c in range(cpg)], axis=0)
                    dsts = _dot(vstack, ke)
                    parts = [None] * cpg
                    for cc in range(cpg):
                        c = cc if d == 0 else cpg - 1 - cc
                        qc = qd[c * CHUNK:(c + 1) * CHUNK]
                        parts[c] = _dot_nt(qc, st.astype(BF16))
                        decay = dec8[sub * cpg + c:sub * cpg + c + 1, :]
                        st = decay * st + dsts[c * HG_DK:(c + 1) * HG_DK]
                    oacc_ref[d, pl.ds(r0, GROUP), cs] = o + jnp.concatenate(parts, axis=0)
                stt_ref[d, h] = st
        return carry

    lax.fori_loop(0, nsteps, body, 0)
    if emit_state:
        for d in range(2):
            for h in range(HG_HEADS):
                st_ref[0, d, h] = stt_ref[d, h].T

    for h in range(HG_HEADS):
        cs = slice(h * HG_DK, (h + 1) * HG_DK)
        o = oacc_ref[0, :, cs] + oacc_ref[1, :, cs]
        ms = jnp.mean(o * o, -1, keepdims=True)
        on = o * lax.rsqrt(ms + RMS_EPS) * ng_ref[:, cs]
        og_ref[:, cs] = (on * sg_ref[:, cs]).astype(BF16)


def _hgrn_call(prep, norm_g, s0, seq, emit_state):
    qdf, kdf, kef, qdb, kdb, keb, v, vt, dec, sg = prep
    n = v.shape[0]
    nb = n // seq
    has_init = s0 is not None
    tok_spec = pl.BlockSpec((seq, HG_W), lambda b: (b, 0))
    in_specs = [tok_spec] * 7 + [
        pl.BlockSpec((seq // GROUP, HG_W, GROUP), lambda b: (b, 0, 0)),
        pl.BlockSpec((seq // CHUNK, 2 * HG_W), lambda b: (b, 0)),
        tok_spec,
        pl.BlockSpec((1, HG_W), lambda b: (0, 0)),
    ]
    args = [qdf, kdf, kef, qdb, kdb, keb, v, vt, dec, sg, norm_g]
    st_block = (1, 2, HG_HEADS, HG_DK, HG_DK)
    if has_init:
        in_specs.append(pl.BlockSpec(st_block, lambda b: (b, 0, 0, 0, 0)))
        args.append(s0)
    out_specs = [tok_spec]
    out_shape = [jax.ShapeDtypeStruct((n, HG_W), BF16)]
    if emit_state:
        out_specs.append(pl.BlockSpec(st_block, lambda b: (b, 0, 0, 0, 0)))
        out_shape.append(jax.ShapeDtypeStruct((nb, 2, HG_HEADS, HG_DK, HG_DK), F32))
    res = pl.pallas_call(
        functools.partial(_hgrn_kernel, seq=seq, has_init=has_init, emit_state=emit_state),
        grid=(nb,),
        in_specs=in_specs,
        out_specs=out_specs,
        out_shape=out_shape,
        scratch_shapes=[pltpu.VMEM((2, seq, HG_W), F32),
                        pltpu.VMEM((2, HG_HEADS, HG_DK, HG_DK), F32)],
        compiler_params=_cparams(1, 16 * seq * HG_W * 4 + (8 << 20)),
        name="hgrn",
    )(*args)
    return res if emit_state else (res[0], None)


def _shift_rows(x, off, nrows, row):
    if off == 0:
        return x
    y = pltpu.roll(x, (-off) % nrows, axis=0)
    ok = (row + off >= 0) & (row + off < nrows)
    return jnp.where(ok, y, 0.0)


def _scan_rows(a, u, reverse):
    n = a.shape[0]
    row = lax.broadcasted_iota(jnp.int32, a.shape, 0)
    s = 1
    while s < n:
        if reverse:
            a_sh = pltpu.roll(a, n - s, axis=0)
            u_sh = pltpu.roll(u, n - s, axis=0)
            ok = row < n - s
        else:
            a_sh = pltpu.roll(a, s, axis=0)
            u_sh = pltpu.roll(u, s, axis=0)
            ok = row >= s
        u = jnp.where(ok, a * u_sh + u, u)
        a = jnp.where(ok, a * a_sh, a)
        s *= 2
    return a, u


def _rglru_kernel(*refs, seq, grid_rows, has_init, emit_state):
    xr_ref, ggr_ref, cw_ref, cb_ref, wg_ref, bg_ref, lam_ref = refs[:7]
    pos = 7
    h0_ref = None
    if has_init:
        h0_ref = refs[pos]
        pos += 1
    y_ref = refs[pos]
    pos += 1
    st_ref = None
    if emit_state:
        st_ref = refs[pos]
        pos += 1
    a_ref, u_ref, hsum_ref = refs[pos:pos + 3]

    stride = GRID_W if grid_rows else 1
    row = lax.broadcasted_iota(jnp.int32, (seq, RG_W), 0)
    x = xr_ref[...]
    xc = jnp.zeros_like(x) + cb_ref[...]
    for j in range(CONV_W):
        xc = xc + cw_ref[j:j + 1, :] * _shift_rows(x, (j - CONV_LEFT) * stride, seq, row)
    gates = jax.nn.sigmoid(_dot(xc.astype(BF16), wg_ref[...]) + bg_ref[...])

    lam = lam_ref[...]
    nl = -lam
    softplus = jnp.maximum(nl, 0.0) + jnp.log1p(jnp.exp(-jnp.abs(nl)))

    for d in range(2):
        r = gates[:, (2 * d) * RG_W:(2 * d + 1) * RG_W]
        i = gates[:, (2 * d + 1) * RG_W:(2 * d + 2) * RG_W]
        log_a = -RG_C * r * softplus[d:d + 1, :]
        a = jnp.exp(log_a)
        u = jnp.sqrt(jnp.tanh(-log_a) * (a * a + 1.0)) * (i * xc)
        if has_init:
            h0 = h0_ref[0, d:d + 1, :]
        else:
            h0 = jnp.zeros((1, RG_W), F32)
        rev = d == 1
        if not grid_rows:
            acum, hz = _scan_rows(a, u, rev)
            h = hz + acum * h0
            last = h[0:1, :] if rev else h[seq - 1:seq, :]
        else:
            nrow = seq // GRID_W
            a_ref[...] = a
            u_ref[...] = u
            order = range(nrow - 1, -1, -1) if rev else range(nrow)
            hl = jnp.zeros((GRID_W, RG_W), F32)
            ac = jnp.ones((GRID_W, RG_W), F32)
            for rr in order:
                sl = slice(rr * GRID_W, (rr + 1) * GRID_W)
                ar = a_ref[sl, :]
                hl = ar * hl + u_ref[sl, :]
                ac = ar * ac
                u_ref[sl, :] = hl
                a_ref[sl, :] = ac
            ccum, hend0 = _scan_rows(ac, hl, rev)
            hend = hend0 + ccum * h0
            crow = lax.broadcasted_iota(jnp.int32, (GRID_W, RG_W), 0)
            if rev:
                hin = jnp.where(crow == GRID_W - 1, h0, pltpu.roll(hend, GRID_W - 1, axis=0))
                last = hend[0:1, :]
            else:
                hin = jnp.where(crow == 0, h0, pltpu.roll(hend, 1, axis=0))
                last = hend[GRID_W - 1:GRID_W, :]
            for rr in range(nrow):
                sl = slice(rr * GRID_W, (rr + 1) * GRID_W)
                u_ref[sl, :] = u_ref[sl, :] + a_ref[sl, :] * hin
            h = u_ref[...]
        if d == 0:
            hsum_ref[...] = h
        else:
            hsum_ref[...] = hsum_ref[...] + h
        if emit_state:
            st_ref[0, d:d + 1, :] = last
    y_ref[...] = (hsum_ref[...] * ggr_ref[...]).astype(BF16)


def _rglru_call(xr, ggr, conv_w, conv_b, wg, bg, lam, h0, seq, grid_rows, emit_state):
    n = xr.shape[0]
    nb = n // seq
    has_init = h0 is not None
    tok_spec = pl.BlockSpec((seq, RG_W), lambda b: (b, 0))
    full = lambda a: pl.BlockSpec(a.shape, lambda b: (0,) * a.ndim)
    in_specs = [tok_spec, tok_spec, full(conv_w), full(conv_b), full(wg), full(bg), full(lam)]
    args = [xr, ggr, conv_w, conv_b, wg, bg, lam]
    if has_init:
        in_specs.append(pl.BlockSpec((1, 2, RG_W), lambda b: (b, 0, 0)))
        args.append(h0)
    out_specs = [tok_spec]
    out_shape = [jax.ShapeDtypeStruct((n, RG_W), BF16)]
    if emit_state:
        out_specs.append(pl.BlockSpec((1, 2, RG_W), lambda b: (b, 0, 0)))
        out_shape.append(jax.ShapeDtypeStruct((nb, 2, RG_W), F32))
    res = pl.pallas_call(
        functools.partial(_rglru_kernel, seq=seq, grid_rows=grid_rows, has_init=has_init,
                          emit_state=emit_state),
        grid=(nb,),
        in_specs=in_specs,
        out_specs=out_specs,
        out_shape=out_shape,
        scratch_shapes=[pltpu.VMEM((seq, RG_W), F32)] * 3,
        compiler_params=_cparams(1, 20 * seq * RG_W * 4 + (8 << 20)),
        name="rglru",
    )(*args)
    return res if emit_state else (res[0], None)


def _oproj_kernel(og_ref, yrg_ref, x_ref, mod_ref, wo_ref, g_ref, b_ref, x1_ref, h2t_ref):
    mod = mod_ref[0]
    g1 = mod[:, 2 * D_MODEL:3 * D_MODEL]
    sh2 = mod[:, 3 * D_MODEL:4 * D_MODEL]
    sc2 = mod[:, 4 * D_MODEL:5 * D_MODEL]
    mix = _dot(og_ref[...], wo_ref[0:HG_W, :]) + _dot(yrg_ref[...], wo_ref[HG_W:, :])
    x1 = _layer_norm(ALPHA * x_ref[...] + g1 * mix, g_ref[...], b_ref[...])
    x1_ref[...] = x1
    h2 = x1 * (1.0 + sc2) + sh2
    h2t_ref[...] = h2.T.astype(BF16)


def _oproj_call(og, yrg, x2d, mod3, w_out_bf, ln_g, ln_b, mod_row_fn):
    n = x2d.shape[0]
    nt = n // TOK_TILE
    return pl.pallas_call(
        _oproj_kernel,
        grid=(nt,),
        in_specs=[
            pl.BlockSpec((TOK_TILE, HG_W), lambda t: (t, 0)),
            pl.BlockSpec((TOK_TILE, RG_W), lambda t: (t, 0)),
            pl.BlockSpec((TOK_TILE, D_MODEL), lambda t: (t, 0)),
            pl.BlockSpec((1, 1, 6 * D_MODEL), lambda t: (mod_row_fn(t), 0, 0)),
            pl.BlockSpec((D_MODEL, D_MODEL), lambda t: (0, 0)),
            pl.BlockSpec((1, D_MODEL), lambda t: (0, 0)),
            pl.BlockSpec((1, D_MODEL), lambda t: (0, 0)),
        ],
        out_specs=[
            pl.BlockSpec((TOK_TILE, D_MODEL), lambda t: (t, 0)),
            pl.BlockSpec((D_MODEL, TOK_TILE), lambda t: (0, t)),
        ],
        out_shape=[
            jax.ShapeDtypeStruct((n, D_MODEL), F32),
            jax.ShapeDtypeStruct((D_MODEL, n), BF16),
        ],
        compiler_params=_cparams(1, 32 << 20),
        name="oproj",
    )(og, yrg, x2d, mod3, w_out_bf, ln_g, ln_b)


def _cmpx(vals, i, j):
    hi = jnp.maximum(vals[i], vals[j])
    lo = jnp.minimum(vals[i], vals[j])
    vals[i] = hi
    vals[j] = lo


def _bitonic_merge_desc(vals):
    n = len(vals)
    d = n // 2
    while d >= 1:
        for i in range(n):
            if (i & d) == 0:
                _cmpx(vals, i, i + d)
        d //= 2
    return vals


def _sort_desc(vals):
    n = len(vals)
    if n == 1:
        return vals
    top = _sort_desc(vals[:n // 2])
    bot = _sort_desc(vals[n // 2:])
    return _bitonic_merge_desc(top + bot[::-1])


def _merge_top(a, b):
    n = len(a)
    return _bitonic_merge_desc([jnp.maximum(a[i], b[n - 1 - i]) for i in range(n)])


def _merge_sublanes(vals):
    s = SUBLANES // 2
    while s >= 1:
        rolled = [pltpu.roll(v, s, axis=0) for v in vals]
        vals = _merge_top(vals, rolled)
        s //= 2
    return vals


def _top_values(s):
    nv = PEER_NKEYS // SUBLANES
    vals = [s[i * SUBLANES:(i + 1) * SUBLANES, :] for i in range(nv)]
    vals = _sort_desc(vals)
    return _merge_sublanes(vals)


def _count_prefix(pred, rows):
    assert len(rows) == 16
    t16 = pred(rows[15])
    t8 = pred(rows[7])
    t4 = pred(jnp.where(t8, rows[11], rows[3]))
    t2 = pred(jnp.where(t8, jnp.where(t4, rows[13], rows[9]), jnp.where(t4, rows[5], rows[1])))
    hi = jnp.where(t4, jnp.where(t2, rows[14], rows[12]), jnp.where(t2, rows[10], rows[8]))
    lo = jnp.where(t4, jnp.where(t2, rows[6], rows[4]), jnp.where(t2, rows[2], rows[0]))
    t1 = pred(jnp.where(t8, hi, lo))
    cnt = (jnp.where(t8, 8.0, 0.0) + jnp.where(t4, 4.0, 0.0)
           + jnp.where(t2, 2.0, 0.0) + jnp.where(t1, 1.0, 0.0))
    return jnp.where(t16, 16.0, cnt)


def _route_kernel(ht_ref, wqt_ref, keys_ref, nb_ref, c0_ref, r1_ref, e1_ref):
    tm = ht_ref.shape[1]
    qt = _dot(wqt_ref[...], ht_ref[...]).astype(BF16)
    half = PEER_DQ // 2
    sub = lax.broadcasted_iota(jnp.int32, (SUBLANES, tm), 0)
    for h in range(PEER_HEADS):
        s = []
        top = []
        for p in range(2):
            r0 = (2 * h + p) * half
            sp = _dot(keys_ref[2 * h + p], qt[r0:r0 + half, :])
            s.append(sp)
            top.append(_top_values(sp))
        lists = []
        for gb in range(PEER_TOPK // SUBLANES):
            bsel = jnp.zeros((SUBLANES, tm), F32)
            for b in range(SUBLANES):
                bsel = jnp.where(sub == b, top[1][gb * SUBLANES + b], bsel)
            lists.append([top[0][a] + bsel for a in range(PEER_TOPK)])
        cand = lists[0]
        for gb in range(1, len(lists)):
            cand = _merge_top(cand, lists[gb])
        fv = _merge_sublanes(cand)
        mx = fv[0][0:1, :]
        zsum = jnp.zeros((1, tm), F32)
        for kk in range(PEER_TOPK):
            zsum = zsum + jnp.exp(fv[kk][0:1, :] - mx)
        tau = fv[PEER_TOPK - 1][0:1, :]
        sv1 = [top[1][b][0:1, :] for b in range(PEER_TOPK)]
        rb = 4 * SUBLANES
        for r0 in range(0, PEER_NKEYS, rb):
            s0b = s[0][r0:r0 + rb, :]
            s1b = s[1][r0:r0 + rb, :]
            nb_ref[h, r0:r0 + rb, :] = _count_prefix(lambda r: s0b + r >= tau, sv1)
            r1_ref[h, r0:r0 + rb, :] = _count_prefix(lambda r: r > s1b, sv1).astype(BF16)
        e1_ref[h] = jnp.exp(s[1] - top[1][0][0:1, :]).astype(BF16)
        c0_ref[h] = jnp.exp(s[0] - top[0][0][0:1, :]) / zsum


def _route_call(h2t, wqt_bf, keys_bf):
    n = h2t.shape[1]
    nt = n // ROUTE_TM
    big = lambda dt: jax.ShapeDtypeStruct((PEER_HEADS, PEER_NKEYS, n), dt)
    big_spec = pl.BlockSpec((PEER_HEADS, PEER_NKEYS, ROUTE_TM), lambda t: (0, 0, t))
    return pl.pallas_call(
        _route_kernel,
        grid=(nt,),
        in_specs=[
            pl.BlockSpec((D_MODEL, ROUTE_TM), lambda t: (0, t)),
            pl.BlockSpec(wqt_bf.shape, lambda t: (0, 0)),
            pl.BlockSpec(keys_bf.shape, lambda t: (0, 0, 0)),
        ],
        out_specs=[big_spec] * 4,
        out_shape=[big(F32), big(F32), big(BF16), big(BF16)],
        compiler_params=_cparams(1, 40 << 20),
        name="route",
    )(h2t, wqt_bf, keys_bf)


def _peer_kernel(ht_ref, u_ref, vt_ref, nb_ref, c0_ref, r1_ref, e1_ref,
                 x1_ref, mod_ref, g_ref, b_ref, out_ref, pt_ref, wt_ref, acc_ref, re_ref):
    step = pl.program_id(1)
    nsteps = pl.num_programs(1)
    tm = ht_ref.shape[1]
    nloc = PEER_EB // PEER_NKEYS

    @pl.when(step == 0)
    def _():
        acc_ref[...] = jnp.zeros_like(acc_ref)
        re_ref[:, 0, :, 0:tm] = r1_ref[...]
        re_ref[:, 1, :, LANES:LANES + tm] = e1_ref[...]

    pt_ref[...] = _dot(u_ref[...], ht_ref[...]).astype(BF16)

    pack = 2 * SUBLANES
    npk = PEER_NKEYS // pack
    zero = jnp.zeros((npk, pack, LANES), BF16)
    for mt in range(tm // LANES):
        ls = slice(mt * LANES, (mt + 1) * LANES)
        for iq in range(nloc // PEER_IG):
            gates = [zero] * PEER_IG
            for h in range(PEER_HEADS):
                r1 = re_ref[h, 0, :, ls].reshape(npk, pack, LANES)
                e1 = re_ref[h, 1, :, (mt + 1) * LANES:(mt + 2) * LANES].reshape(npk, pack, LANES)
                for k in range(PEER_IG):
                    il = iq * PEER_IG + k
                    nbv = jnp.broadcast_to(nb_ref[h, 0, il:il + 1, ls], (pack, LANES)).astype(BF16)
                    c0v = jnp.broadcast_to(c0_ref[h, 0, il:il + 1, ls], (pack, LANES)).astype(BF16)
                    gates[k] = gates[k] + jnp.where(r1 < nbv[None], e1, zero) * c0v[None]
            for k in range(PEER_IG):
                il = iq * PEER_IG + k
                rs = slice(il * PEER_NKEYS, (il + 1) * PEER_NKEYS)
                wt_ref[rs, ls] = gates[k].reshape(PEER_NKEYS, LANES) * _gelu(pt_ref[rs, ls])
    acc_ref[...] += _dot(vt_ref[0], wt_ref[...])

    @pl.when(step == nsteps - 1)
    def _():
        mod = mod_ref[0]
        g2 = mod[:, 5 * D_MODEL:6 * D_MODEL]
        ff = acc_ref[...].T
        out_ref[...] = _layer_norm(ALPHA * x1_ref[...] + g2 * ff, g_ref[...], b_ref[...])


def _peer_call(h2t, u_bf, vt_bf, route, x1, mod3, ln_g, ln_b, mod_row_fn):
    nb, c0, r1, e1 = route
    n = h2t.shape[1]
    nt = n // PEER_TM
    nblk = u_bf.shape[0] // PEER_EB
    big_spec = pl.BlockSpec((PEER_HEADS, PEER_NKEYS, PEER_TM), lambda t, i: (0, 0, t))
    nloc = PEER_EB // PEER_NKEYS
    nb = nb.reshape(PEER_HEADS, PEER_NKEYS // nloc, nloc, n)
    c0 = c0.reshape(PEER_HEADS, PEER_NKEYS // nloc, nloc, n)
    row_spec = pl.BlockSpec((PEER_HEADS, 1, nloc, PEER_TM), lambda t, i: (0, i, 0, t))
    return pl.pallas_call(
        _peer_kernel,
        grid=(nt, nblk),
        in_specs=[
            pl.BlockSpec((D_MODEL, PEER_TM), lambda t, i: (0, t)),
            pl.BlockSpec((PEER_EB, D_MODEL), lambda t, i: (i, 0)),
            pl.BlockSpec((1, D_MODEL, PEER_EB), lambda t, i: (i, 0, 0)),
            row_spec, row_spec, big_spec, big_spec,
            pl.BlockSpec((PEER_TM, D_MODEL), lambda t, i: (t, 0)),
            pl.BlockSpec((1, 1, 6 * D_MODEL), lambda t, i: (mod_row_fn(t), 0, 0)),
            pl.BlockSpec((1, D_MODEL), lambda t, i: (0, 0)),
            pl.BlockSpec((1, D_MODEL), lambda t, i: (0, 0)),
        ],
        out_specs=pl.BlockSpec((PEER_TM, D_MODEL), lambda t, i: (t, 0)),
        out_shape=jax.ShapeDtypeStruct((n, D_MODEL), F32),
        scratch_shapes=[
            pltpu.VMEM((PEER_EB, PEER_TM), BF16),
            pltpu.VMEM((PEER_EB, PEER_TM), BF16),
            pltpu.VMEM((D_MODEL, PEER_TM), F32),
            pltpu.VMEM((PEER_HEADS, 2, PEER_NKEYS, PEER_TM + LANES), BF16),
        ],
        compiler_params=_cparams(2, 52 << 20),
        name="peer",
    )(h2t, u_bf, vt_bf, nb, c0, r1, e1, x1, mod3, ln_g, ln_b)


def _block_diag(w):
    nh, bw, _ = w.shape
    eye = jnp.eye(nh, dtype=w.dtype)
    return (eye[:, None, :, None] * w[:, :, None, :]).reshape(nh * bw, nh * bw)


def _run_path(x, mod3, weights, s_hg0, s_rg0, seq, grid_rows, emit_state, mod_tok_row):
    bsz = x.shape[0]
    x2d = x.reshape(bsz * seq, D_MODEL)
    row256 = lambda t: mod_tok_row(t, TOK_TILE)
    prep = _inproj_call(x2d, mod3, weights["w_in"], weights["lb"], row256)
    qdf, kdf, kef, qdb, kdb, keb, v, vt, dec, sg, xr, ggr = prep
    og, st_hg = _hgrn_call((qdf, kdf, kef, qdb, kdb, keb, v, vt, dec, sg), weights["hg_norm"],
                           s_hg0, seq, emit_state)
    yrg, st_rg = _rglru_call(xr, ggr, weights["conv_w"], weights["conv_b"], weights["wg"],
                             weights["bg"], weights["lam"], s_rg0, seq, grid_rows, emit_state)
    x1, h2t = _oproj_call(og, yrg, x2d, mod3, weights["w_out"], weights["ln1_g"], weights["ln1_b"],
                          row256)
    route = _route_call(h2t, weights["wqt"], weights["keys"])
    out = _peer_call(h2t, weights["u"], weights["vt"], route, x1, mod3, weights["ln2_g"],
                     weights["ln2_b"], lambda t: mod_tok_row(t, PEER_TM))
    return out.reshape(bsz, seq, D_MODEL), st_hg, st_rg


def kernel(x_prompt, x_sample, c, state_hgrn, state_rglru, c_ctx, w_ada, b_ada, w_in, hgrn_lb,
           hgrn_norm_g, conv_w, conv_b, rg_wr, rg_br, rg_wi, rg_bi, rg_lam, w_out, ln1_g, ln1_b,
           peer_wq, peer_keys, peer_u, peer_v, ln2_g, ln2_b):
    assert w_ada.shape[0] == DEPTH
    bp, seq_p, _ = x_prompt.shape
    bs, seq_s, _ = x_sample.shape
    l = 0
    nrows = -(-(1 + bs) // SUBLANES) * SUBLANES
    cond = jnp.zeros((nrows, D_MODEL), F32).at[0].set(c_ctx).at[1:1 + bs].set(c)
    mod = _mod_call(cond, w_ada[l], b_ada[l][None, :])
    mod3 = mod[:, None, :]

    wg = jnp.concatenate([_block_diag(rg_wr[l, 0]), _block_diag(rg_wi[l, 0]),
                          _block_diag(rg_wr[l, 1]), _block_diag(rg_wi[l, 1])], axis=1)
    bg = jnp.concatenate([rg_br[l, 0], rg_bi[l, 0], rg_br[l, 1], rg_bi[l, 1]])[None, :]
    weights = {
        "w_in": w_in[l].astype(BF16),
        "lb": hgrn_lb,
        "hg_norm": hgrn_norm_g[l][None, :],
        "conv_w": conv_w[l],
        "conv_b": conv_b[l][None, :],
        "wg": wg.astype(BF16),
        "bg": bg,
        "lam": rg_lam[l],
        "w_out": w_out[l].astype(BF16),
        "ln1_g": ln1_g[l][None, :],
        "ln1_b": ln1_b[l][None, :],
        "wqt": peer_wq[l].T.astype(BF16),
        "keys": peer_keys[l].reshape(PEER_HEADS * 2, PEER_NKEYS, PEER_DQ // 2).astype(BF16),
        "u": peer_u[l].astype(BF16),
        "vt": peer_v[l].reshape(-1, PEER_EB, D_MODEL).transpose(0, 2, 1).astype(BF16),
        "ln2_g": ln2_g[l][None, :],
        "ln2_b": ln2_b[l][None, :],
    }

    yp, st_hg, st_rg = _run_path(x_prompt, mod3, weights, None, None, seq_p, False, True,
                                 lambda t, tile: 0)
    ys, _, _ = _run_path(x_sample, mod3, weights, state_hgrn[:, l], state_rglru[:, l], seq_s, True,
                         False, lambda t, tile: 1 + (t * tile) // seq_s)
    new_hg = st_hg[:, None].astype(x_prompt.dtype)
    new_rg = st_rg[:, None].astype(x_prompt.dtype)
    return (yp, ys, new_hg, new_rg)
```

```python
import functools

import jax
import jax.numpy as jnp
from jax import lax
from jax.experimental import pallas as pl
from jax.experimental.pallas import tpu as pltpu

F32 = jnp.float32
BF16 = jnp.bfloat16

D_MODEL = 1024
HG_HEADS = 4
HG_DK = 128
HG_W = 512
RG_W = 512
RG_HEADS = 8
RG_BW = 64
RG_C = 8.0
CONV_W = 4
CONV_LEFT = CONV_W // 2
GRID_W = 64
CHUNK = 32
D_IN = 5 * HG_W + 2 * RG_W
PEER_HEADS = 8
PEER_NKEYS = 128
PEER_TOPK = 16
PEER_DQ = 256
DEPTH = 1
ALPHA = (2.0 * DEPTH) ** 0.25
LN_EPS = 1e-5
RMS_EPS = 1e-6

LANES = 128
SUBLANES = 8
VMEM_CAP_BYTES = 56 * 1024 * 1024

TOK_TILE = 256
GROUP = 128
GPS = SUBLANES * CHUNK // GROUP
PEER_TM = 512
PEER_EB = 1024
PEER_IG = 4
ROUTE_TM = 256


def _cparams(n_axes, vmem_bytes, flags=None):
    return pltpu.CompilerParams(
        dimension_semantics=("arbitrary",) * n_axes,
        vmem_limit_bytes=min(int(vmem_bytes), VMEM_CAP_BYTES),
        flags=flags,
    )


def _silu(x):
    return x * jax.nn.sigmoid(x)


def _gelu(x):
    return jax.nn.gelu(x, approximate=True)


def _dot(a, b):
    return jnp.dot(a, b, preferred_element_type=F32)


def _dot_nt(a, b):
    return lax.dot_general(a, b, (((1,), (1,)), ((), ())), preferred_element_type=F32)


def _layer_norm(x, g, b):
    mu = jnp.mean(x, -1, keepdims=True)
    xc = x - mu
    var = jnp.mean(xc * xc, -1, keepdims=True)
    return xc * lax.rsqrt(var + LN_EPS) * g + b


def _mod_kernel(cond_ref, w_ref, b_ref, o_ref):
    c = _silu(cond_ref[...]).astype(BF16)
    o_ref[...] = _dot(c, w_ref[...].astype(BF16)) + b_ref[...]


def _mod_call(cond, w_ada, b_ada):
    rows = cond.shape[0]
    nblk = w_ada.shape[1] // D_MODEL
    return pl.pallas_call(
        _mod_kernel,
        grid=(nblk,),
        in_specs=[
            pl.BlockSpec((rows, D_MODEL), lambda j: (0, 0)),
            pl.BlockSpec((D_MODEL, D_MODEL), lambda j: (0, j)),
            pl.BlockSpec((1, D_MODEL), lambda j: (0, j)),
        ],
        out_specs=pl.BlockSpec((rows, D_MODEL), lambda j: (0, j)),
        out_shape=jax.ShapeDtypeStruct((rows, w_ada.shape[1]), F32),
        compiler_params=_cparams(1, 24 << 20),
        name="mod",
    )(cond, w_ada, b_ada)


def _mask01(cond):
    return jnp.where(cond, 1.0, 0.0).astype(BF16)


def _split2(x):
    hi = x.astype(BF16)
    lo = (x - hi.astype(F32)).astype(BF16)
    return hi, lo


def _inproj_kernel(x_ref, mod_ref, w_ref, lb_ref,
                   qdf_ref, kdf_ref, kef_ref, qdb_ref, kdb_ref, keb_ref,
                   v_ref, vt_ref, dec_ref, sg_ref, xr_ref, ggr_ref):
    tt = x_ref.shape[0]
    mod = mod_ref[0]
    sh1 = mod[:, 0:D_MODEL]
    sc1 = mod[:, D_MODEL:2 * D_MODEL]
    h = (x_ref[...] * (1.0 + sc1) + sh1).astype(BF16)
    z = _dot(h, w_ref[...])
    q = z[:, 0:HG_W]
    iv = z[:, HG_W:2 * HG_W]
    g = z[:, 4 * HG_W:5 * HG_W]
    xr = z[:, 5 * HG_W:5 * HG_W + RG_W]
    gr = z[:, 5 * HG_W + RG_W:]
    qs = _silu(q)
    ivb = iv.astype(BF16)
    v_ref[...] = ivb
    ivt = iv.T.astype(BF16)
    for gi in range(tt // GROUP):
        vt_ref[gi] = ivt[:, gi * GROUP:(gi + 1) * GROUP]
    sg_ref[...] = _silu(g)
    xr_ref[...] = xr
    ggr_ref[...] = _gelu(gr)

    lbp = lb_ref[...]
    mx = jnp.max(lbp, axis=1, keepdims=True)
    e = jnp.exp(lbp - mx)
    lb_all = e[:, 0, :] / jnp.sum(e, axis=1)

    row = lax.broadcasted_iota(jnp.int32, (tt, tt), 0)
    col = lax.broadcasted_iota(jnp.int32, (tt, tt), 1)
    same = (row // CHUNK) == (col // CHUNK)
    tot = _mask01(same)
    srow = lax.broadcasted_iota(jnp.int32, (tt // CHUNK, tt), 0)
    scol = lax.broadcasted_iota(jnp.int32, (tt // CHUNK, tt), 1)
    sel = _mask01(srow == scol // CHUNK)

    outs = ((qdf_ref, kdf_ref, kef_ref), (qdb_ref, kdb_ref, keb_ref))
    for d in range(2):
        zf = z[:, (2 + d) * HG_W:(3 + d) * HG_W]
        lb = lb_all[d:d + 1, :]
        sig = jax.nn.sigmoid(zf)
        f = lb + (1.0 - lb) * sig
        k = (1.0 - lb) * (1.0 - sig)
        lf = jnp.log(f)
        lf_hi, lf_lo = _split2(lf)
        tri = _mask01(same & ((col <= row) if d == 0 else (col >= row)))
        b = _dot(tri, lf_hi) + _dot(tri, lf_lo)
        btot = _dot(tot, lf_hi) + _dot(tot, lf_lo)
        qd_ref, kd_ref, ke_ref = outs[d]
        qd_ref[...] = (qs * jnp.exp(b)).astype(BF16)
        kd_ref[...] = (k * jnp.exp(-b)).astype(BF16)
        ke_ref[...] = (k * jnp.exp(btot - b)).astype(BF16)
        dec_ref[:, d * HG_W:(d + 1) * HG_W] = jnp.exp(_dot(sel, lf_hi) + _dot(sel, lf_lo))


def _inproj_call(x2d, mod3, w_in_bf, lb_p, mod_row_fn):
    n = x2d.shape[0]
    nt = n // TOK_TILE
    tok_bf = lambda: jax.ShapeDtypeStruct((n, HG_W), BF16)
    tok_f = lambda: jax.ShapeDtypeStruct((n, HG_W), F32)
    tok_spec = pl.BlockSpec((TOK_TILE, HG_W), lambda t: (t, 0))
    gpt = TOK_TILE // GROUP
    cpt = TOK_TILE // CHUNK
    return pl.pallas_call(
        _inproj_kernel,
        grid=(nt,),
        in_specs=[
            pl.BlockSpec((TOK_TILE, D_MODEL), lambda t: (t, 0)),
            pl.BlockSpec((1, 1, 6 * D_MODEL), lambda t: (mod_row_fn(t), 0, 0)),
            pl.BlockSpec((D_MODEL, D_IN), lambda t: (0, 0)),
            pl.BlockSpec(lb_p.shape, lambda t: (0, 0, 0)),
        ],
        out_specs=[tok_spec] * 7 + [
            pl.BlockSpec((gpt, HG_W, GROUP), lambda t: (t, 0, 0)),
            pl.BlockSpec((cpt, 2 * HG_W), lambda t: (t, 0)),
            tok_spec, tok_spec, tok_spec,
        ],
        out_shape=[tok_bf() for _ in range(7)] + [
            jax.ShapeDtypeStruct((n // GROUP, HG_W, GROUP), BF16),
            jax.ShapeDtypeStruct((n // CHUNK, 2 * HG_W), F32),
            tok_f(), tok_f(), tok_f(),
        ],
        compiler_params=_cparams(1, 48 << 20),
        name="inproj",
    )(x2d, mod3, w_in_bf, lb_p)


def _hgrn_kernel(*refs, seq, has_init, emit_state):
    (qdf_ref, kdf_ref, kef_ref, qdb_ref, kdb_ref, keb_ref, v_ref, vt_ref, dec_ref, sg_ref,
     ng_ref) = refs[:11]
    pos = 11
    s0_ref = None
    if has_init:
        s0_ref = refs[pos]
        pos += 1
    og_ref = refs[pos]
    pos += 1
    st_ref = None
    if emit_state:
        st_ref = refs[pos]
        pos += 1
    oacc_ref, stt_ref = refs[pos:pos + 2]

    ngroups = seq // GROUP
    cpg = GROUP // CHUNK
    assert GPS * cpg == SUBLANES and ngroups % GPS == 0
    nsteps = ngroups // GPS
    row = lax.broadcasted_iota(jnp.int32, (GROUP, GROUP), 0)
    col = lax.broadcasted_iota(jnp.int32, (GROUP, GROUP), 1)
    same = (row // CHUNK) == (col // CHUNK)
    dirs = ((qdf_ref, kdf_ref, kef_ref), (qdb_ref, kdb_ref, keb_ref))
    for d in range(2):
        for h in range(HG_HEADS):
            if has_init:
                stt_ref[d, h] = s0_ref[0, d, h].T
            else:
                stt_ref[d, h] = jnp.zeros((HG_DK, HG_DK), F32)

    def body(pi, carry):
        for d in range(2):
            qd_ref, kd_ref, ke_ref = dirs[d]
            keep = same & ((col <= row) if d == 0 else (col >= row))
            pr = pi if d == 0 else nsteps - 1 - pi
            r8 = pl.multiple_of(pr * SUBLANES, SUBLANES)
            for h in range(HG_HEADS):
                cs = slice(h * HG_DK, (h + 1) * HG_DK)
                dec8 = dec_ref[pl.ds(r8, SUBLANES), d * HG_W + h * HG_DK:d * HG_W + (h + 1) * HG_DK]
                st = stt_ref[d, h]
                for gg in range(GPS):
                    sub = gg if d == 0 else GPS - 1 - gg
                    g = pr * GPS + sub
                    r0 = pl.multiple_of(g * GROUP, GROUP)
                    qd = qd_ref[pl.ds(r0, GROUP), cs]
                    kd = kd_ref[pl.ds(r0, GROUP), cs]
                    ke = ke_ref[pl.ds(r0, GROUP), cs]
                    vg = v_ref[pl.ds(r0, GROUP), cs]
                    vtg = vt_ref[g, cs, :]
                    att = jnp.where(keep, _dot_nt(qd, kd), 0.0).astype(BF16)
                    o = _dot(att, vg)
                    vstack = jnp.concatenate(
                        [vtg * _mask01((col // CHUNK) == c) for c in range(cpg)], axis=0)
                    dsts = _dot(vstack, ke)
                    parts = [None] * cpg
                    for cc in range(cpg):
                        c = cc if d == 0 else cpg - 1 - cc
                        qc = qd[c * CHUNK:(c + 1) * CHUNK]
                        parts[c] = _dot_nt(qc, st.astype(BF16))
                        decay = dec8[sub * cpg + c:sub * cpg + c + 1, :]
                        st = decay * st + dsts[c * HG_DK:(c + 1) * HG_DK]
                    oacc_ref[d, pl.ds(r0, GROUP), cs] = o + jnp.concatenate(parts, axis=0)
                stt_ref[d, h] = st
        return carry

    lax.fori_loop(0, nsteps, body, 0)
    if emit_state:
        for d in range(2):
            for h in range(HG_HEADS):
                st_ref[0, d, h] = stt_ref[d, h].T

    for h in range(HG_HEADS):
        cs = slice(h * HG_DK, (h + 1) * HG_DK)
        o = oacc_ref[0, :, cs] + oacc_ref[1, :, cs]
        ms = jnp.mean(o * o, -1, keepdims=True)
        on = o * lax.rsqrt(ms + RMS_EPS) * ng_ref[:, cs]
        og_ref[:, cs] = (on * sg_ref[:, cs]).astype(BF16)


def _hgrn_call(prep, norm_g, s0, seq, emit_state):
    qdf, kdf, kef, qdb, kdb, keb, v, vt, dec, sg = prep
    n = v.shape[0]
    nb = n // seq
    has_init = s0 is not None
    tok_spec = pl.BlockSpec((seq, HG_W), lambda b: (b, 0))
    in_specs = [tok_spec] * 7 + [
        pl.BlockSpec((seq // GROUP, HG_W, GROUP), lambda b: (b, 0, 0)),
        pl.BlockSpec((seq // CHUNK, 2 * HG_W), lambda b: (b, 0)),
        tok_spec,
        pl.BlockSpec((1, HG_W), lambda b: (0, 0)),
    ]
    args = [qdf, kdf, kef, qdb, kdb, keb, v, vt, dec, sg, norm_g]
    st_block = (1, 2, HG_HEADS, HG_DK, HG_DK)
    if has_init:
        in_specs.append(pl.BlockSpec(st_block, lambda b: (b, 0, 0, 0, 0)))
        args.append(s0)
    out_specs = [tok_spec]
    out_shape = [jax.ShapeDtypeStruct((n, HG_W), BF16)]
    if emit_state:
        out_specs.append(pl.BlockSpec(st_block, lambda b: (b, 0, 0, 0, 0)))
        out_shape.append(jax.ShapeDtypeStruct((nb, 2, HG_HEADS, HG_DK, HG_DK), F32))
    res = pl.pallas_call(
        functools.partial(_hgrn_kernel, seq=seq, has_init=has_init, emit_state=emit_state),
        grid=(nb,),
        in_specs=in_specs,
        out_specs=out_specs,
        out_shape=out_shape,
        scratch_shapes=[pltpu.VMEM((2, seq, HG_W), F32),
                        pltpu.VMEM((2, HG_HEADS, HG_DK, HG_DK), F32)],
        compiler_params=_cparams(1, 16 * seq * HG_W * 4 + (8 << 20)),
        name="hgrn",
    )(*args)
    return res if emit_state else (res[0], None)


def _shift_rows(x, off, nrows, row):
    if off == 0:
        return x
    y = pltpu.roll(x, (-off) % nrows, axis=0)
    ok = (row + off >= 0) & (row + off < nrows)
    return jnp.where(ok, y, 0.0)


def _scan_rows(a, u, reverse):
    n = a.shape[0]
    row = lax.broadcasted_iota(jnp.int32, a.shape, 0)
    s = 1
    while s < n:
        if reverse:
            a_sh = pltpu.roll(a, n - s, axis=0)
            u_sh = pltpu.roll(u, n - s, axis=0)
            ok = row < n - s
        else:
            a_sh = pltpu.roll(a, s, axis=0)
            u_sh = pltpu.roll(u, s, axis=0)
            ok = row >= s
        u = jnp.where(ok, a * u_sh + u, u)
        a = jnp.where(ok, a * a_sh, a)
        s *= 2
    return a, u


def _rglru_kernel(*refs, seq, grid_rows, has_init, emit_state):
    xr_ref, ggr_ref, cw_ref, cb_ref, wg_ref, bg_ref, lam_ref = refs[:7]
    pos = 7
    h0_ref = None
    if has_init:
        h0_ref = refs[pos]
        pos += 1
    y_ref = refs[pos]
    pos += 1
    st_ref = None
    if emit_state:
        st_ref = refs[pos]
        pos += 1
    a_ref, u_ref, hsum_ref = refs[pos:pos + 3]

    stride = GRID_W if grid_rows else 1
    row = lax.broadcasted_iota(jnp.int32, (seq, RG_W), 0)
    x = xr_ref[...]
    xc = jnp.zeros_like(x) + cb_ref[...]
    for j in range(CONV_W):
        xc = xc + cw_ref[j:j + 1, :] * _shift_rows(x, (j - CONV_LEFT) * stride, seq, row)
    gates = jax.nn.sigmoid(_dot(xc.astype(BF16), wg_ref[...]) + bg_ref[...])

    lam = lam_ref[...]
    nl = -lam
    softplus = jnp.maximum(nl, 0.0) + jnp.log1p(jnp.exp(-jnp.abs(nl)))

    for d in range(2):
        r = gates[:, (2 * d) * RG_W:(2 * d + 1) * RG_W]
        i = gates[:, (2 * d + 1) * RG_W:(2 * d + 2) * RG_W]
        log_a = -RG_C * r * softplus[d:d + 1, :]
        a = jnp.exp(log_a)
        u = jnp.sqrt(jnp.tanh(-log_a) * (a * a + 1.0)) * (i * xc)
        if has_init:
            h0 = h0_ref[0, d:d + 1, :]
        else:
            h0 = jnp.zeros((1, RG_W), F32)
        rev = d == 1
        if not grid_rows:
            acum, hz = _scan_rows(a, u, rev)
            h = hz + acum * h0
            last = h[0:1, :] if rev else h[seq - 1:seq, :]
        else:
            nrow = seq // GRID_W
            a_ref[...] = a
            u_ref[...] = u
            order = range(nrow - 1, -1, -1) if rev else range(nrow)
            hl = jnp.zeros((GRID_W, RG_W), F32)
            ac = jnp.ones((GRID_W, RG_W), F32)
            for rr in order:
                sl = slice(rr * GRID_W, (rr + 1) * GRID_W)
                ar = a_ref[sl, :]
                hl = ar * hl + u_ref[sl, :]
                ac = ar * ac
                u_ref[sl, :] = hl
                a_ref[sl, :] = ac
            ccum, hend0 = _scan_rows(ac, hl, rev)
            hend = hend0 + ccum * h0
            crow = lax.broadcasted_iota(jnp.int32, (GRID_W, RG_W), 0)
            if rev:
                hin = jnp.where(crow == GRID_W - 1, h0, pltpu.roll(hend, GRID_W - 1, axis=0))
                last = hend[0:1, :]
            else:
                hin = jnp.where(crow == 0, h0, pltpu.roll(hend, 1, axis=0))
                last = hend[GRID_W - 1:GRID_W, :]
            for rr in range(nrow):
                sl = slice(rr * GRID_W, (rr + 1) * GRID_W)
                u_ref[sl, :] = u_ref[sl, :] + a_ref[sl, :] * hin
            h = u_ref[...]
        if d == 0:
            hsum_ref[...] = h
        else:
            hsum_ref[...] = hsum_ref[...] + h
        if emit_state:
            st_ref[0, d:d + 1, :] = last
    y_ref[...] = (hsum_ref[...] * ggr_ref[...]).astype(BF16)


def _rglru_call(xr, ggr, conv_w, conv_b, wg, bg, lam, h0, seq, grid_rows, emit_state):
    n = xr.shape[0]
    nb = n // seq
    has_init = h0 is not None
    tok_spec = pl.BlockSpec((seq, RG_W), lambda b: (b, 0))
    full = lambda a: pl.BlockSpec(a.shape, lambda b: (0,) * a.ndim)
    in_specs = [tok_spec, tok_spec, full(conv_w), full(conv_b), full(wg), full(bg), full(lam)]
    args = [xr, ggr, conv_w, conv_b, wg, bg, lam]
    if has_init:
        in_specs.append(pl.BlockSpec((1, 2, RG_W), lambda b: (b, 0, 0)))
        args.append(h0)
    out_specs = [tok_spec]
    out_shape = [jax.ShapeDtypeStruct((n, RG_W), BF16)]
    if emit_state:
        out_specs.append(pl.BlockSpec((1, 2, RG_W), lambda b: (b, 0, 0)))
        out_shape.append(jax.ShapeDtypeStruct((nb, 2, RG_W), F32))
    res = pl.pallas_call(
        functools.partial(_rglru_kernel, seq=seq, grid_rows=grid_rows, has_init=has_init,
                          emit_state=emit_state),
        grid=(nb,),
        in_specs=in_specs,
        out_specs=out_specs,
        out_shape=out_shape,
        scratch_shapes=[pltpu.VMEM((seq, RG_W), F32)] * 3,
        compiler_params=_cparams(1, 20 * seq * RG_W * 4 + (8 << 20)),
        name="rglru",
    )(*args)
    return res if emit_state else (res[0], None)


def _oproj_kernel(og_ref, yrg_ref, x_ref, mod_ref, wo_ref, g_ref, b_ref, x1_ref, h2t_ref):
    mod = mod_ref[0]
    g1 = mod[:, 2 * D_MODEL:3 * D_MODEL]
    sh2 = mod[:, 3 * D_MODEL:4 * D_MODEL]
    sc2 = mod[:, 4 * D_MODEL:5 * D_MODEL]
    mix = _dot(og_ref[...], wo_ref[0:HG_W, :]) + _dot(yrg_ref[...], wo_ref[HG_W:, :])
    x1 = _layer_norm(ALPHA * x_ref[...] + g1 * mix, g_ref[...], b_ref[...])
    x1_ref[...] = x1
    h2 = x1 * (1.0 + sc2) + sh2
    h2t_ref[...] = h2.T.astype(BF16)


def _oproj_call(og, yrg, x2d, mod3, w_out_bf, ln_g, ln_b, mod_row_fn):
    n = x2d.shape[0]
    nt = n // TOK_TILE
    return pl.pallas_call(
        _oproj_kernel,
        grid=(nt,),
        in_specs=[
            pl.BlockSpec((TOK_TILE, HG_W), lambda t: (t, 0)),
            pl.BlockSpec((TOK_TILE, RG_W), lambda t: (t, 0)),
            pl.BlockSpec((TOK_TILE, D_MODEL), lambda t: (t, 0)),
            pl.BlockSpec((1, 1, 6 * D_MODEL), lambda t: (mod_row_fn(t), 0, 0)),
            pl.BlockSpec((D_MODEL, D_MODEL), lambda t: (0, 0)),
            pl.BlockSpec((1, D_MODEL), lambda t: (0, 0)),
            pl.BlockSpec((1, D_MODEL), lambda t: (0, 0)),
        ],
        out_specs=[
            pl.BlockSpec((TOK_TILE, D_MODEL), lambda t: (t, 0)),
            pl.BlockSpec((D_MODEL, TOK_TILE), lambda t: (0, t)),
        ],
        out_shape=[
            jax.ShapeDtypeStruct((n, D_MODEL), F32),
            jax.ShapeDtypeStruct((D_MODEL, n), BF16),
        ],
        compiler_params=_cparams(1, 32 << 20),
        name="oproj",
    )(og, yrg, x2d, mod3, w_out_bf, ln_g, ln_b)


def _cmpx(vals, i, j):
    hi = jnp.maximum(vals[i], vals[j])
    lo = jnp.minimum(vals[i], vals[j])
    vals[i] = hi
    vals[j] = lo


def _bitonic_merge_desc(vals):
    n = len(vals)
    d = n // 2
    while d >= 1:
        for i in range(n):
            if (i & d) == 0:
                _cmpx(vals, i, i + d)
        d //= 2
    return vals


def _sort_desc(vals):
    n = len(vals)
    if n == 1:
        return vals
    top = _sort_desc(vals[:n // 2])
    bot = _sort_desc(vals[n // 2:])
    return _bitonic_merge_desc(top + bot[::-1])


def _merge_top(a, b):
    n = len(a)
    return _bitonic_merge_desc([jnp.maximum(a[i], b[n - 1 - i]) for i in range(n)])


def _merge_sublanes(vals):
    s = SUBLANES // 2
    while s >= 1:
        rolled = [pltpu.roll(v, s, axis=0) for v in vals]
        vals = _merge_top(vals, rolled)
        s //= 2
    return vals


def _top_values(s):
    nv = PEER_NKEYS // SUBLANES
    vals = [s[i * SUBLANES:(i + 1) * SUBLANES, :] for i in range(nv)]
    vals = _sort_desc(vals)
    return _merge_sublanes(vals)


def _count_prefix(pred, rows):
    assert len(rows) == 16
    t16 = pred(rows[15])
    t8 = pred(rows[7])
    t4 = pred(jnp.where(t8, rows[11], rows[3]))
    t2 = pred(jnp.where(t8, jnp.where(t4, rows[13], rows[9]), jnp.where(t4, rows[5], rows[1])))
    hi = jnp.where(t4, jnp.where(t2, rows[14], rows[12]), jnp.where(t2, rows[10], rows[8]))
    lo = jnp.where(t4, jnp.where(t2, rows[6], rows[4]), jnp.where(t2, rows[2], rows[0]))
    t1 = pred(jnp.where(t8, hi, lo))
    cnt = (jnp.where(t8, 8.0, 0.0) + jnp.where(t4, 4.0, 0.0)
           + jnp.where(t2, 2.0, 0.0) + jnp.where(t1, 1.0, 0.0))
    return jnp.where(t16, 16.0, cnt)


def _route_kernel(ht_ref, wqt_ref, keys_ref, nb_ref, c0_ref, r1_ref, e1_ref):
    tm = ht_ref.shape[1]
    qt = _dot(wqt_ref[...], ht_ref[...]).astype(BF16)
    half = PEER_DQ // 2
    sub = lax.broadcasted_iota(jnp.int32, (SUBLANES, tm), 0)
    for h in range(PEER_HEADS):
        s = []
        top = []
        for p in range(2):
            r0 = (2 * h + p) * half
            sp = _dot(keys_ref[2 * h + p], qt[r0:r0 + half, :])
            s.append(sp)
            top.append(_top_values(sp))
        lists = []
        for gb in range(PEER_TOPK // SUBLANES):
            bsel = jnp.zeros((SUBLANES, tm), F32)
            for b in range(SUBLANES):
                bsel = jnp.where(sub == b, top[1][gb * SUBLANES + b], bsel)
            lists.append([top[0][a] + bsel for a in range(PEER_TOPK)])
        cand = lists[0]
        for gb in range(1, len(lists)):
            cand = _merge_top(cand, lists[gb])
        fv = _merge_sublanes(cand)
        mx = fv[0][0:1, :]
        zsum = jnp.zeros((1, tm), F32)
        for kk in range(PEER_TOPK):
            zsum = zsum + jnp.exp(fv[kk][0:1, :] - mx)
        tau = fv[PEER_TOPK - 1][0:1, :]
        sv1 = [top[1][b][0:1, :] for b in range(PEER_TOPK)]
        rb = 4 * SUBLANES
        for r0 in range(0, PEER_NKEYS, rb):
            s0b = s[0][r0:r0 + rb, :]
            s1b = s[1][r0:r0 + rb, :]
            nb_ref[h, r0:r0 + rb, :] = _count_prefix(lambda r: s0b + r >= tau, sv1)
            r1_ref[h, r0:r0 + rb, :] = _count_prefix(lambda r: r > s1b, sv1).astype(BF16)
        e1_ref[h] = jnp.exp(s[1] - top[1][0][0:1, :]).astype(BF16)
        c0_ref[h] = jnp.exp(s[0] - top[0][0][0:1, :]) / zsum


def _route_call(h2t, wqt_bf, keys_bf):
    n = h2t.shape[1]
    nt = n // ROUTE_TM
    big = lambda dt: jax.ShapeDtypeStruct((PEER_HEADS, PEER_NKEYS, n), dt)
    big_spec = pl.BlockSpec((PEER_HEADS, PEER_NKEYS, ROUTE_TM), lambda t: (0, 0, t))
    return pl.pallas_call(
        _route_kernel,
        grid=(nt,),
        in_specs=[
            pl.BlockSpec((D_MODEL, ROUTE_TM), lambda t: (0, t)),
            pl.BlockSpec(wqt_bf.shape, lambda t: (0, 0)),
            pl.BlockSpec(keys_bf.shape, lambda t: (0, 0, 0)),
        ],
        out_specs=[big_spec] * 4,
        out_shape=[big(F32), big(F32), big(BF16), big(BF16)],
        compiler_params=_cparams(1, 40 << 20),
        name="route",
    )(h2t, wqt_bf, keys_bf)


def _peer_kernel(ht_ref, u_ref, vt_ref, nb_ref, c0_ref, r1_ref, e1_ref,
                 x1_ref, mod_ref, g_ref, b_ref, out_ref, pt_ref, wt_ref, acc_ref, re_ref):
    step = pl.program_id(1)
    nsteps = pl.num_programs(1)
    tm = ht_ref.shape[1]
    nloc = PEER_EB // PEER_NKEYS

    @pl.when(step == 0)
    def _():
        acc_ref[...] = jnp.zeros_like(acc_ref)
        re_ref[:, 0, :, 0:tm] = r1_ref[...]
        re_ref[:, 1, :, LANES:LANES + tm] = e1_ref[...]

    pt_ref[...] = _dot(u_ref[...].astype(BF16), ht_ref[...]).astype(BF16)

    pack = 2 * SUBLANES
    npk = PEER_NKEYS // pack
    zero = jnp.zeros((npk, pack, LANES), BF16)
    for mt in range(tm // LANES):
        ls = slice(mt * LANES, (mt + 1) * LANES)
        for iq in range(nloc // PEER_IG):
            gates = [zero] * PEER_IG
            for h in range(PEER_HEADS):
                r1 = re_ref[h, 0, :, ls].reshape(npk, pack, LANES)
                e1 = re_ref[h, 1, :, (mt + 1) * LANES:(mt + 2) * LANES].reshape(npk, pack, LANES)
                for k in range(PEER_IG):
                    il = iq * PEER_IG + k
                    nbv = jnp.broadcast_to(nb_ref[h, 0, il:il + 1, ls], (pack, LANES)).astype(BF16)
                    c0v = jnp.broadcast_to(c0_ref[h, 0, il:il + 1, ls], (pack, LANES)).astype(BF16)
                    gates[k] = gates[k] + jnp.where(r1 < nbv[None], e1, zero) * c0v[None]
            for k in range(PEER_IG):
                il = iq * PEER_IG + k
                rs = slice(il * PEER_NKEYS, (il + 1) * PEER_NKEYS)
                wt_ref[rs, ls] = gates[k].reshape(PEER_NKEYS, LANES) * _gelu(pt_ref[rs, ls])
    acc_ref[...] += _dot(vt_ref[0], wt_ref[...])

    @pl.when(step == nsteps - 1)
    def _():
        mod = mod_ref[0]
        g2 = mod[:, 5 * D_MODEL:6 * D_MODEL]
        ff = acc_ref[...].T
        out_ref[...] = _layer_norm(ALPHA * x1_ref[...] + g2 * ff, g_ref[...], b_ref[...])


def _peer_call(h2t, u_bf, vt_bf, route, x1, mod3, ln_g, ln_b, mod_row_fn):
    nb, c0, r1, e1 = route
    n = h2t.shape[1]
    nt = n // PEER_TM
    nblk = u_bf.shape[0] // PEER_EB
    big_spec = pl.BlockSpec((PEER_HEADS, PEER_NKEYS, PEER_TM), lambda t, i: (0, 0, t))
    nloc = PEER_EB // PEER_NKEYS
    nb = nb.reshape(PEER_HEADS, PEER_NKEYS // nloc, nloc, n)
    c0 = c0.reshape(PEER_HEADS, PEER_NKEYS // nloc, nloc, n)
    row_spec = pl.BlockSpec((PEER_HEADS, 1, nloc, PEER_TM), lambda t, i: (0, i, 0, t))
    return pl.pallas_call(
        _peer_kernel,
        grid=(nt, nblk),
        in_specs=[
            pl.BlockSpec((D_MODEL, PEER_TM), lambda t, i: (0, t)),
            pl.BlockSpec((PEER_EB, D_MODEL), lambda t, i: (i, 0)),
            pl.BlockSpec((1, D_MODEL, PEER_EB), lambda t, i: (i, 0, 0)),
            row_spec, row_spec, big_spec, big_spec,
            pl.BlockSpec((PEER_TM, D_MODEL), lambda t, i: (t, 0)),
            pl.BlockSpec((1, 1, 6 * D_MODEL), lambda t, i: (mod_row_fn(t), 0, 0)),
            pl.BlockSpec((1, D_MODEL), lambda t, i: (0, 0)),
            pl.BlockSpec((1, D_MODEL), lambda t, i: (0, 0)),
        ],
        out_specs=pl.BlockSpec((PEER_TM, D_MODEL), lambda t, i: (t, 0)),
        out_shape=jax.ShapeDtypeStruct((n, D_MODEL), F32),
        scratch_shapes=[
            pltpu.VMEM((PEER_EB, PEER_TM), BF16),
            pltpu.VMEM((PEER_EB, PEER_TM), BF16),
            pltpu.VMEM((D_MODEL, PEER_TM), F32),
            pltpu.VMEM((PEER_HEADS, 2, PEER_NKEYS, PEER_TM + LANES), BF16),
        ],
        compiler_params=_cparams(2, 52 << 20),
        name="peer",
    )(h2t, u_bf, vt_bf, nb, c0, r1, e1, x1, mod3, ln_g, ln_b)


def _block_diag(w):
    nh, bw, _ = w.shape
    eye = jnp.eye(nh, dtype=w.dtype)
    return (eye[:, None, :, None] * w[:, :, None, :]).reshape(nh * bw, nh * bw)


def _run_path(x, mod3, weights, s_hg0, s_rg0, seq, grid_rows, emit_state, mod_tok_row):
    bsz = x.shape[0]
    x2d = x.reshape(bsz * seq, D_MODEL)
    row256 = lambda t: mod_tok_row(t, TOK_TILE)
    prep = _inproj_call(x2d, mod3, weights["w_in"], weights["lb"], row256)
    qdf, kdf, kef, qdb, kdb, keb, v, vt, dec, sg, xr, ggr = prep
    og, st_hg = _hgrn_call((qdf, kdf, kef, qdb, kdb, keb, v, vt, dec, sg), weights["hg_norm"],
                           s_hg0, seq, emit_state)
    yrg, st_rg = _rglru_call(xr, ggr, weights["conv_w"], weights["conv_b"], weights["wg"],
                             weights["bg"], weights["lam"], s_rg0, seq, grid_rows, emit_state)
    x1, h2t = _oproj_call(og, yrg, x2d, mod3, weights["w_out"], weights["ln1_g"], weights["ln1_b"],
                          row256)
    route = _route_call(h2t, weights["wqt"], weights["keys"])
    out = _peer_call(h2t, weights["u"], weights["vt"], route, x1, mod3, weights["ln2_g"],
                     weights["ln2_b"], lambda t: mod_tok_row(t, PEER_TM))
    return out.reshape(bsz, seq, D_MODEL), st_hg, st_rg


def kernel(x_prompt, x_sample, c, state_hgrn, state_rglru, c_ctx, w_ada, b_ada, w_in, hgrn_lb,
           hgrn_norm_g, conv_w, conv_b, rg_wr, rg_br, rg_wi, rg_bi, rg_lam, w_out, ln1_g, ln1_b,
           peer_wq, peer_keys, peer_u, peer_v, ln2_g, ln2_b):
    assert w_ada.shape[0] == DEPTH
    bp, seq_p, _ = x_prompt.shape
    bs, seq_s, _ = x_sample.shape
    l = 0
    nrows = -(-(1 + bs) // SUBLANES) * SUBLANES
    cond = jnp.zeros((nrows, D_MODEL), F32).at[0].set(c_ctx).at[1:1 + bs].set(c)
    mod = _mod_call(cond, w_ada[l], b_ada[l][None, :])
    mod3 = mod[:, None, :]

    wg = jnp.concatenate([_block_diag(rg_wr[l, 0]), _block_diag(rg_wi[l, 0]),
                          _block_diag(rg_wr[l, 1]), _block_diag(rg_wi[l, 1])], axis=1)
    bg = jnp.concatenate([rg_br[l, 0], rg_bi[l, 0], rg_br[l, 1], rg_bi[l, 1]])[None, :]
    weights = {
        "w_in": w_in[l].astype(BF16),
        "lb": hgrn_lb,
        "hg_norm": hgrn_norm_g[l][None, :],
        "conv_w": conv_w[l],
        "conv_b": conv_b[l][None, :],
        "wg": wg.astype(BF16),
        "bg": bg,
        "lam": rg_lam[l],
        "w_out": w_out[l].astype(BF16),
        "ln1_g": ln1_g[l][None, :],
        "ln1_b": ln1_b[l][None, :],
        "wqt": peer_wq[l].T.astype(BF16),
        "keys": peer_keys[l].reshape(PEER_HEADS * 2, PEER_NKEYS, PEER_DQ // 2).astype(BF16),
        "u": peer_u[l],
        "vt": peer_v[l].reshape(-1, PEER_EB, D_MODEL).transpose(0, 2, 1).astype(BF16),
        "ln2_g": ln2_g[l][None, :],
        "ln2_b": ln2_b[l][None, :],
    }

    yp, st_hg, st_rg = _run_path(x_prompt, mod3, weights, None, None, seq_p, False, True,
                                 lambda t, tile: 0)
    ys, _, _ = _run_path(x_sample, mod3, weights, state_hgrn[:, l], state_rglru[:, l], seq_s, True,
                         False, lambda t, tile: 1 + (t * tile) // seq_s)
    new_hg = st_hg[:, None].astype(x_prompt.dtype)
    new_rg = st_rg[:, None].astype(x_prompt.dtype)
    return (yp, ys, new_hg, new_rg)
```

```python
import functools

import jax
import jax.numpy as jnp
from jax import lax
from jax.experimental import pallas as pl
from jax.experimental.pallas import tpu as pltpu

F32 = jnp.float32
BF16 = jnp.bfloat16

D_MODEL = 1024
HG_HEADS = 4
HG_DK = 128
HG_W = 512
RG_W = 512
RG_HEADS = 8
RG_BW = 64
RG_C = 8.0
CONV_W = 4
CONV_LEFT = CONV_W // 2
GRID_W = 64
CHUNK = 32
D_IN = 5 * HG_W + 2 * RG_W
PEER_HEADS = 8
PEER_NKEYS = 128
PEER_TOPK = 16
PEER_DQ = 256
DEPTH = 1
ALPHA = (2.0 * DEPTH) ** 0.25
LN_EPS = 1e-5
RMS_EPS = 1e-6

LANES = 128
SUBLANES = 8
VMEM_CAP_BYTES = 56 * 1024 * 1024

TOK_TILE = 256
GROUP = 128
GPS = SUBLANES * CHUNK // GROUP
PEER_TM = 512
PEER_EB = 2048
PEER_IG = 4
ROUTE_TM = 256


def _cparams(n_axes, vmem_bytes, flags=None):
    return pltpu.CompilerParams(
        dimension_semantics=("arbitrary",) * n_axes,
        vmem_limit_bytes=min(int(vmem_bytes), VMEM_CAP_BYTES),
        flags=flags,
    )


def _silu(x):
    return x * jax.nn.sigmoid(x)


def _gelu(x):
    return jax.nn.gelu(x, approximate=True)


def _dot(a, b):
    return jnp.dot(a, b, preferred_element_type=F32)


def _dot_nt(a, b):
    return lax.dot_general(a, b, (((1,), (1,)), ((), ())), preferred_element_type=F32)


def _layer_norm(x, g, b):
    mu = jnp.mean(x, -1, keepdims=True)
    xc = x - mu
    var = jnp.mean(xc * xc, -1, keepdims=True)
    return xc * lax.rsqrt(var + LN_EPS) * g + b


def _mod_kernel(cond_ref, w_ref, b_ref, o_ref):
    c = _silu(cond_ref[...]).astype(BF16)
    o_ref[...] = _dot(c, w_ref[...].astype(BF16)) + b_ref[...]


def _mod_call(cond, w_ada, b_ada):
    rows = cond.shape[0]
    nblk = w_ada.shape[1] // D_MODEL
    return pl.pallas_call(
        _mod_kernel,
        grid=(nblk,),
        in_specs=[
            pl.BlockSpec((rows, D_MODEL), lambda j: (0, 0)),
            pl.BlockSpec((D_MODEL, D_MODEL), lambda j: (0, j)),
            pl.BlockSpec((1, D_MODEL), lambda j: (0, j)),
        ],
        out_specs=pl.BlockSpec((rows, D_MODEL), lambda j: (0, j)),
        out_shape=jax.ShapeDtypeStruct((rows, w_ada.shape[1]), F32),
        compiler_params=_cparams(1, 24 << 20),
        name="mod",
    )(cond, w_ada, b_ada)


def _mask01(cond):
    return jnp.where(cond, 1.0, 0.0).astype(BF16)


def _split2(x):
    hi = x.astype(BF16)
    lo = (x - hi.astype(F32)).astype(BF16)
    return hi, lo


def _inproj_kernel(x_ref, mod_ref, w_ref, lb_ref,
                   qdf_ref, kdf_ref, kef_ref, qdb_ref, kdb_ref, keb_ref,
                   v_ref, vt_ref, dec_ref, sg_ref, xr_ref, ggr_ref):
    tt = x_ref.shape[0]
    mod = mod_ref[0]
    sh1 = mod[:, 0:D_MODEL]
    sc1 = mod[:, D_MODEL:2 * D_MODEL]
    h = (x_ref[...] * (1.0 + sc1) + sh1).astype(BF16)
    z = _dot(h, w_ref[...])
    q = z[:, 0:HG_W]
    iv = z[:, HG_W:2 * HG_W]
    g = z[:, 4 * HG_W:5 * HG_W]
    xr = z[:, 5 * HG_W:5 * HG_W + RG_W]
    gr = z[:, 5 * HG_W + RG_W:]
    qs = _silu(q)
    ivb = iv.astype(BF16)
    v_ref[...] = ivb
    ivt = iv.T.astype(BF16)
    for gi in range(tt // GROUP):
        vt_ref[gi] = ivt[:, gi * GROUP:(gi + 1) * GROUP]
    sg_ref[...] = _silu(g)
    xr_ref[...] = xr
    ggr_ref[...] = _gelu(gr)

    lbp = lb_ref[...]
    mx = jnp.max(lbp, axis=1, keepdims=True)
    e = jnp.exp(lbp - mx)
    lb_all = e[:, 0, :] / jnp.sum(e, axis=1)

    row = lax.broadcasted_iota(jnp.int32, (tt, tt), 0)
    col = lax.broadcasted_iota(jnp.int32, (tt, tt), 1)
    same = (row // CHUNK) == (col // CHUNK)
    tot = _mask01(same)
    srow = lax.broadcasted_iota(jnp.int32, (tt // CHUNK, tt), 0)
    scol = lax.broadcasted_iota(jnp.int32, (tt // CHUNK, tt), 1)
    sel = _mask01(srow == scol // CHUNK)

    outs = ((qdf_ref, kdf_ref, kef_ref), (qdb_ref, kdb_ref, keb_ref))
    for d in range(2):
        zf = z[:, (2 + d) * HG_W:(3 + d) * HG_W]
        lb = lb_all[d:d + 1, :]
        sig = jax.nn.sigmoid(zf)
        f = lb + (1.0 - lb) * sig
        k = (1.0 - lb) * (1.0 - sig)
        lf = jnp.log(f)
        lf_hi, lf_lo = _split2(lf)
        tri = _mask01(same & ((col <= row) if d == 0 else (col >= row)))
        b = _dot(tri, lf_hi) + _dot(tri, lf_lo)
        btot = _dot(tot, lf_hi) + _dot(tot, lf_lo)
        qd_ref, kd_ref, ke_ref = outs[d]
        qd_ref[...] = (qs * jnp.exp(b)).astype(BF16)
        kd_ref[...] = (k * jnp.exp(-b)).astype(BF16)
        ke_ref[...] = (k * jnp.exp(btot - b)).astype(BF16)
        dec_ref[:, d * HG_W:(d + 1) * HG_W] = jnp.exp(_dot(sel, lf_hi) + _dot(sel, lf_lo))


def _inproj_call(x2d, mod3, w_in_bf, lb_p, mod_row_fn):
    n = x2d.shape[0]
    nt = n // TOK_TILE
    tok_bf = lambda: jax.ShapeDtypeStruct((n, HG_W), BF16)
    tok_f = lambda: jax.ShapeDtypeStruct((n, HG_W), F32)
    tok_spec = pl.BlockSpec((TOK_TILE, HG_W), lambda t: (t, 0))
    gpt = TOK_TILE // GROUP
    cpt = TOK_TILE // CHUNK
    return pl.pallas_call(
        _inproj_kernel,
        grid=(nt,),
        in_specs=[
            pl.BlockSpec((TOK_TILE, D_MODEL), lambda t: (t, 0)),
            pl.BlockSpec((1, 1, 6 * D_MODEL), lambda t: (mod_row_fn(t), 0, 0)),
            pl.BlockSpec((D_MODEL, D_IN), lambda t: (0, 0)),
            pl.BlockSpec(lb_p.shape, lambda t: (0, 0, 0)),
        ],
        out_specs=[tok_spec] * 7 + [
            pl.BlockSpec((gpt, HG_W, GROUP), lambda t: (t, 0, 0)),
            pl.BlockSpec((cpt, 2 * HG_W), lambda t: (t, 0)),
            tok_spec, tok_spec, tok_spec,
        ],
        out_shape=[tok_bf() for _ in range(7)] + [
            jax.ShapeDtypeStruct((n // GROUP, HG_W, GROUP), BF16),
            jax.ShapeDtypeStruct((n // CHUNK, 2 * HG_W), F32),
            tok_f(), tok_f(), tok_f(),
        ],
        compiler_params=_cparams(1, 48 << 20),
        name="inproj",
    )(x2d, mod3, w_in_bf, lb_p)


def _hgrn_kernel(*refs, seq, has_init, emit_state):
    (qdf_ref, kdf_ref, kef_ref, qdb_ref, kdb_ref, keb_ref, v_ref, vt_ref, dec_ref, sg_ref,
     ng_ref) = refs[:11]
    pos = 11
    s0_ref = None
    if has_init:
        s0_ref = refs[pos]
        pos += 1
    og_ref = refs[pos]
    pos += 1
    st_ref = None
    if emit_state:
        st_ref = refs[pos]
        pos += 1
    oacc_ref, stt_ref = refs[pos:pos + 2]

    ngroups = seq // GROUP
    cpg = GROUP // CHUNK
    assert GPS * cpg == SUBLANES and ngroups % GPS == 0
    nsteps = ngroups // GPS
    row = lax.broadcasted_iota(jnp.int32, (GROUP, GROUP), 0)
    col = lax.broadcasted_iota(jnp.int32, (GROUP, GROUP), 1)
    same = (row // CHUNK) == (col // CHUNK)
    dirs = ((qdf_ref, kdf_ref, kef_ref), (qdb_ref, kdb_ref, keb_ref))
    for d in range(2):
        for h in range(HG_HEADS):
            if has_init:
                stt_ref[d, h] = s0_ref[0, d, h].T
            else:
                stt_ref[d, h] = jnp.zeros((HG_DK, HG_DK), F32)

    def body(pi, carry):
        for d in range(2):
            qd_ref, kd_ref, ke_ref = dirs[d]
            keep = same & ((col <= row) if d == 0 else (col >= row))
            pr = pi if d == 0 else nsteps - 1 - pi
            r8 = pl.multiple_of(pr * SUBLANES, SUBLANES)
            for h in range(HG_HEADS):
                cs = slice(h * HG_DK, (h + 1) * HG_DK)
                dec8 = dec_ref[pl.ds(r8, SUBLANES), d * HG_W + h * HG_DK:d * HG_W + (h + 1) * HG_DK]
                st = stt_ref[d, h]
                for gg in range(GPS):
                    sub = gg if d == 0 else GPS - 1 - gg
                    g = pr * GPS + sub
                    r0 = pl.multiple_of(g * GROUP, GROUP)
                    qd = qd_ref[pl.ds(r0, GROUP), cs]
                    kd = kd_ref[pl.ds(r0, GROUP), cs]
                    ke = ke_ref[pl.ds(r0, GROUP), cs]
                    vg = v_ref[pl.ds(r0, GROUP), cs]
                    vtg = vt_ref[g, cs, :]
                    att = jnp.where(keep, _dot_nt(qd, kd), 0.0).astype(BF16)
                    o = _dot(att, vg)
                    vstack = jnp.concatenate(
                        [vtg * _mask01((col // CHUNK) == c) for c in range(cpg)], axis=0)
                    dsts = _dot(vstack, ke)
                    parts = [None] * cpg
                    for cc in range(cpg):
                        c = cc if d == 0 else cpg - 1 - cc
                        qc = qd[c * CHUNK:(c + 1) * CHUNK]
                        parts[c] = _dot_nt(qc, st.astype(BF16))
                        decay = dec8[sub * cpg + c:sub * cpg + c + 1, :]
                        st = decay * st + dsts[c * HG_DK:(c + 1) * HG_DK]
                    oacc_ref[d, pl.ds(r0, GROUP), cs] = o + jnp.concatenate(parts, axis=0)
                stt_ref[d, h] = st
        return carry

    lax.fori_loop(0, nsteps, body, 0)
    if emit_state:
        for d in range(2):
            for h in range(HG_HEADS):
                st_ref[0, d, h] = stt_ref[d, h].T

    for h in range(HG_HEADS):
        cs = slice(h * HG_DK, (h + 1) * HG_DK)
        o = oacc_ref[0, :, cs] + oacc_ref[1, :, cs]
        ms = jnp.mean(o * o, -1, keepdims=True)
        on = o * lax.rsqrt(ms + RMS_EPS) * ng_ref[:, cs]
        og_ref[:, cs] = (on * sg_ref[:, cs]).astype(BF16)


def _hgrn_call(prep, norm_g, s0, seq, emit_state):
    qdf, kdf, kef, qdb, kdb, keb, v, vt, dec, sg = prep
    n = v.shape[0]
    nb = n // seq
    has_init = s0 is not None
    tok_spec = pl.BlockSpec((seq, HG_W), lambda b: (b, 0))
    in_specs = [tok_spec] * 7 + [
        pl.BlockSpec((seq // GROUP, HG_W, GROUP), lambda b: (b, 0, 0)),
        pl.BlockSpec((seq // CHUNK, 2 * HG_W), lambda b: (b, 0)),
        tok_spec,
        pl.BlockSpec((1, HG_W), lambda b: (0, 0)),
    ]
    args = [qdf, kdf, kef, qdb, kdb, keb, v, vt, dec, sg, norm_g]
    st_block = (1, 2, HG_HEADS, HG_DK, HG_DK)
    if has_init:
        in_specs.append(pl.BlockSpec(st_block, lambda b: (b, 0, 0, 0, 0)))
        args.append(s0)
    out_specs = [tok_spec]
    out_shape = [jax.ShapeDtypeStruct((n, HG_W), BF16)]
    if emit_state:
        out_specs.append(pl.BlockSpec(st_block, lambda b: (b, 0, 0, 0, 0)))
        out_shape.append(jax.ShapeDtypeStruct((nb, 2, HG_HEADS, HG_DK, HG_DK), F32))
    res = pl.pallas_call(
        functools.partial(_hgrn_kernel, seq=seq, has_init=has_init, emit_state=emit_state),
        grid=(nb,),
        in_specs=in_specs,
        out_specs=out_specs,
        out_shape=out_shape,
        scratch_shapes=[pltpu.VMEM((2, seq, HG_W), F32),
                        pltpu.VMEM((2, HG_HEADS, HG_DK, HG_DK), F32)],
        compiler_params=_cparams(1, 16 * seq * HG_W * 4 + (8 << 20)),
        name="hgrn",
    )(*args)
    return res if emit_state else (res[0], None)


def _shift_rows(x, off, nrows, row):
    if off == 0:
        return x
    y = pltpu.roll(x, (-off) % nrows, axis=0)
    ok = (row + off >= 0) & (row + off < nrows)
    return jnp.where(ok, y, 0.0)


def _scan_rows(a, u, reverse):
    n = a.shape[0]
    row = lax.broadcasted_iota(jnp.int32, a.shape, 0)
    s = 1
    while s < n:
        if reverse:
            a_sh = pltpu.roll(a, n - s, axis=0)
            u_sh = pltpu.roll(u, n - s, axis=0)
            ok = row < n - s
        else:
            a_sh = pltpu.roll(a, s, axis=0)
            u_sh = pltpu.roll(u, s, axis=0)
            ok = row >= s
        u = jnp.where(ok, a * u_sh + u, u)
        a = jnp.where(ok, a * a_sh, a)
        s *= 2
    return a, u


def _rglru_kernel(*refs, seq, grid_rows, has_init, emit_state):
    xr_ref, ggr_ref, cw_ref, cb_ref, wg_ref, bg_ref, lam_ref = refs[:7]
    pos = 7
    h0_ref = None
    if has_init:
        h0_ref = refs[pos]
        pos += 1
    y_ref = refs[pos]
    pos += 1
    st_ref = None
    if emit_state:
        st_ref = refs[pos]
        pos += 1
    a_ref, u_ref, hsum_ref = refs[pos:pos + 3]

    stride = GRID_W if grid_rows else 1
    row = lax.broadcasted_iota(jnp.int32, (seq, RG_W), 0)
    x = xr_ref[...]
    xc = jnp.zeros_like(x) + cb_ref[...]
    for j in range(CONV_W):
        xc = xc + cw_ref[j:j + 1, :] * _shift_rows(x, (j - CONV_LEFT) * stride, seq, row)
    gates = jax.nn.sigmoid(_dot(xc.astype(BF16), wg_ref[...]) + bg_ref[...])

    lam = lam_ref[...]
    nl = -lam
    softplus = jnp.maximum(nl, 0.0) + jnp.log1p(jnp.exp(-jnp.abs(nl)))

    for d in range(2):
        r = gates[:, (2 * d) * RG_W:(2 * d + 1) * RG_W]
        i = gates[:, (2 * d + 1) * RG_W:(2 * d + 2) * RG_W]
        log_a = -RG_C * r * softplus[d:d + 1, :]
        a = jnp.exp(log_a)
        u = jnp.sqrt(jnp.tanh(-log_a) * (a * a + 1.0)) * (i * xc)
        if has_init:
            h0 = h0_ref[0, d:d + 1, :]
        else:
            h0 = jnp.zeros((1, RG_W), F32)
        rev = d == 1
        if not grid_rows:
            acum, hz = _scan_rows(a, u, rev)
            h = hz + acum * h0
            last = h[0:1, :] if rev else h[seq - 1:seq, :]
        else:
            nrow = seq // GRID_W
            a_ref[...] = a
            u_ref[...] = u
            order = range(nrow - 1, -1, -1) if rev else range(nrow)
            hl = jnp.zeros((GRID_W, RG_W), F32)
            ac = jnp.ones((GRID_W, RG_W), F32)
            for rr in order:
                sl = slice(rr * GRID_W, (rr + 1) * GRID_W)
                ar = a_ref[sl, :]
                hl = ar * hl + u_ref[sl, :]
                ac = ar * ac
                u_ref[sl, :] = hl
                a_ref[sl, :] = ac
            ccum, hend0 = _scan_rows(ac, hl, rev)
            hend = hend0 + ccum * h0
            crow = lax.broadcasted_iota(jnp.int32, (GRID_W, RG_W), 0)
            if rev:
                hin = jnp.where(crow == GRID_W - 1, h0, pltpu.roll(hend, GRID_W - 1, axis=0))
                last = hend[0:1, :]
            else:
                hin = jnp.where(crow == 0, h0, pltpu.roll(hend, 1, axis=0))
                last = hend[GRID_W - 1:GRID_W, :]
            for rr in range(nrow):
                sl = slice(rr * GRID_W, (rr + 1) * GRID_W)
                u_ref[sl, :] = u_ref[sl, :] + a_ref[sl, :] * hin
            h = u_ref[...]
        if d == 0:
            hsum_ref[...] = h
        else:
            hsum_ref[...] = hsum_ref[...] + h
        if emit_state:
            st_ref[0, d:d + 1, :] = last
    y_ref[...] = (hsum_ref[...] * ggr_ref[...]).astype(BF16)


def _rglru_call(xr, ggr, conv_w, conv_b, wg, bg, lam, h0, seq, grid_rows, emit_state):
    n = xr.shape[0]
    nb = n // seq
    has_init = h0 is not None
    tok_spec = pl.BlockSpec((seq, RG_W), lambda b: (b, 0))
    full = lambda a: pl.BlockSpec(a.shape, lambda b: (0,) * a.ndim)
    in_specs = [tok_spec, tok_spec, full(conv_w), full(conv_b), full(wg), full(bg), full(lam)]
    args = [xr, ggr, conv_w, conv_b, wg, bg, lam]
    if has_init:
        in_specs.append(pl.BlockSpec((1, 2, RG_W), lambda b: (b, 0, 0)))
        args.append(h0)
    out_specs = [tok_spec]
    out_shape = [jax.ShapeDtypeStruct((n, RG_W), BF16)]
    if emit_state:
        out_specs.append(pl.BlockSpec((1, 2, RG_W), lambda b: (b, 0, 0)))
        out_shape.append(jax.ShapeDtypeStruct((nb, 2, RG_W), F32))
    res = pl.pallas_call(
        functools.partial(_rglru_kernel, seq=seq, grid_rows=grid_rows, has_init=has_init,
                          emit_state=emit_state),
        grid=(nb,),
        in_specs=in_specs,
        out_specs=out_specs,
        out_shape=out_shape,
        scratch_shapes=[pltpu.VMEM((seq, RG_W), F32)] * 3,
        compiler_params=_cparams(1, 20 * seq * RG_W * 4 + (8 << 20)),
        name="rglru",
    )(*args)
    return res if emit_state else (res[0], None)


def _oproj_kernel(og_ref, yrg_ref, x_ref, mod_ref, wo_ref, g_ref, b_ref, x1_ref, h2t_ref):
    mod = mod_ref[0]
    g1 = mod[:, 2 * D_MODEL:3 * D_MODEL]
    sh2 = mod[:, 3 * D_MODEL:4 * D_MODEL]
    sc2 = mod[:, 4 * D_MODEL:5 * D_MODEL]
    mix = _dot(og_ref[...], wo_ref[0:HG_W, :]) + _dot(yrg_ref[...], wo_ref[HG_W:, :])
    x1 = _layer_norm(ALPHA * x_ref[...] + g1 * mix, g_ref[...], b_ref[...])
    x1_ref[...] = x1
    h2 = x1 * (1.0 + sc2) + sh2
    h2t_ref[...] = h2.T.astype(BF16)


def _oproj_call(og, yrg, x2d, mod3, w_out_bf, ln_g, ln_b, mod_row_fn):
    n = x2d.shape[0]
    nt = n // TOK_TILE
    return pl.pallas_call(
        _oproj_kernel,
        grid=(nt,),
        in_specs=[
            pl.BlockSpec((TOK_TILE, HG_W), lambda t: (t, 0)),
            pl.BlockSpec((TOK_TILE, RG_W), lambda t: (t, 0)),
            pl.BlockSpec((TOK_TILE, D_MODEL), lambda t: (t, 0)),
            pl.BlockSpec((1, 1, 6 * D_MODEL), lambda t: (mod_row_fn(t), 0, 0)),
            pl.BlockSpec((D_MODEL, D_MODEL), lambda t: (0, 0)),
            pl.BlockSpec((1, D_MODEL), lambda t: (0, 0)),
            pl.BlockSpec((1, D_MODEL), lambda t: (0, 0)),
        ],
        out_specs=[
            pl.BlockSpec((TOK_TILE, D_MODEL), lambda t: (t, 0)),
            pl.BlockSpec((D_MODEL, TOK_TILE), lambda t: (0, t)),
        ],
        out_shape=[
            jax.ShapeDtypeStruct((n, D_MODEL), F32),
            jax.ShapeDtypeStruct((D_MODEL, n), BF16),
        ],
        compiler_params=_cparams(1, 32 << 20),
        name="oproj",
    )(og, yrg, x2d, mod3, w_out_bf, ln_g, ln_b)


def _cmpx(vals, i, j):
    hi = jnp.maximum(vals[i], vals[j])
    lo = jnp.minimum(vals[i], vals[j])
    vals[i] = hi
    vals[j] = lo


def _bitonic_merge_desc(vals):
    n = len(vals)
    d = n // 2
    while d >= 1:
        for i in range(n):
            if (i & d) == 0:
                _cmpx(vals, i, i + d)
        d //= 2
    return vals


def _sort_desc(vals):
    n = len(vals)
    if n == 1:
        return vals
    top = _sort_desc(vals[:n // 2])
    bot = _sort_desc(vals[n // 2:])
    return _bitonic_merge_desc(top + bot[::-1])


def _merge_top(a, b):
    n = len(a)
    return _bitonic_merge_desc([jnp.maximum(a[i], b[n - 1 - i]) for i in range(n)])


def _merge_sublanes(vals):
    s = SUBLANES // 2
    while s >= 1:
        rolled = [pltpu.roll(v, s, axis=0) for v in vals]
        vals = _merge_top(vals, rolled)
        s //= 2
    return vals


def _top_values(s):
    nv = PEER_NKEYS // SUBLANES
    vals = [s[i * SUBLANES:(i + 1) * SUBLANES, :] for i in range(nv)]
    vals = _sort_desc(vals)
    return _merge_sublanes(vals)


def _count_prefix(pred, rows):
    assert len(rows) == 16
    t16 = pred(rows[15])
    t8 = pred(rows[7])
    t4 = pred(jnp.where(t8, rows[11], rows[3]))
    t2 = pred(jnp.where(t8, jnp.where(t4, rows[13], rows[9]), jnp.where(t4, rows[5], rows[1])))
    hi = jnp.where(t4, jnp.where(t2, rows[14], rows[12]), jnp.where(t2, rows[10], rows[8]))
    lo = jnp.where(t4, jnp.where(t2, rows[6], rows[4]), jnp.where(t2, rows[2], rows[0]))
    t1 = pred(jnp.where(t8, hi, lo))
    cnt = (jnp.where(t8, 8.0, 0.0) + jnp.where(t4, 4.0, 0.0)
           + jnp.where(t2, 2.0, 0.0) + jnp.where(t1, 1.0, 0.0))
    return jnp.where(t16, 16.0, cnt)


def _route_kernel(ht_ref, wqt_ref, keys_ref, nb_ref, c0_ref, r1_ref, e1_ref):
    tm = ht_ref.shape[1]
    qt = _dot(wqt_ref[...], ht_ref[...]).astype(BF16)
    half = PEER_DQ // 2
    sub = lax.broadcasted_iota(jnp.int32, (SUBLANES, tm), 0)
    for h in range(PEER_HEADS):
        s = []
        top = []
        for p in range(2):
            r0 = (2 * h + p) * half
            sp = _dot(keys_ref[2 * h + p], qt[r0:r0 + half, :])
            s.append(sp)
            top.append(_top_values(sp))
        lists = []
        for gb in range(PEER_TOPK // SUBLANES):
            bsel = jnp.zeros((SUBLANES, tm), F32)
            for b in range(SUBLANES):
                bsel = jnp.where(sub == b, top[1][gb * SUBLANES + b], bsel)
            lists.append([top[0][a] + bsel for a in range(PEER_TOPK)])
        cand = lists[0]
        for gb in range(1, len(lists)):
            cand = _merge_top(cand, lists[gb])
        fv = _merge_sublanes(cand)
        mx = fv[0][0:1, :]
        zsum = jnp.zeros((1, tm), F32)
        for kk in range(PEER_TOPK):
            zsum = zsum + jnp.exp(fv[kk][0:1, :] - mx)
        tau = fv[PEER_TOPK - 1][0:1, :]
        sv1 = [top[1][b][0:1, :] for b in range(PEER_TOPK)]
        rb = 4 * SUBLANES
        for r0 in range(0, PEER_NKEYS, rb):
            s0b = s[0][r0:r0 + rb, :]
            s1b = s[1][r0:r0 + rb, :]
            nb_ref[h, r0:r0 + rb, :] = _count_prefix(lambda r: s0b + r >= tau, sv1)
            r1_ref[h, r0:r0 + rb, :] = _count_prefix(lambda r: r > s1b, sv1).astype(BF16)
        e1_ref[h] = jnp.exp(s[1] - top[1][0][0:1, :]).astype(BF16)
        c0_ref[h] = jnp.exp(s[0] - top[0][0][0:1, :]) / zsum


def _route_call(h2t, wqt_bf, keys_bf):
    n = h2t.shape[1]
    nt = n // ROUTE_TM
    big = lambda dt: jax.ShapeDtypeStruct((PEER_HEADS, PEER_NKEYS, n), dt)
    big_spec = pl.BlockSpec((PEER_HEADS, PEER_NKEYS, ROUTE_TM), lambda t: (0, 0, t))
    return pl.pallas_call(
        _route_kernel,
        grid=(nt,),
        in_specs=[
            pl.BlockSpec((D_MODEL, ROUTE_TM), lambda t: (0, t)),
            pl.BlockSpec(wqt_bf.shape, lambda t: (0, 0)),
            pl.BlockSpec(keys_bf.shape, lambda t: (0, 0, 0)),
        ],
        out_specs=[big_spec] * 4,
        out_shape=[big(F32), big(F32), big(BF16), big(BF16)],
        compiler_params=_cparams(1, 40 << 20),
        name="route",
    )(h2t, wqt_bf, keys_bf)


def _peer_kernel(ht_ref, u_ref, vt_ref, nb_ref, c0_ref, r1_ref, e1_ref,
                 x1_ref, mod_ref, g_ref, b_ref, out_ref, pt_ref, wt_ref, acc_ref, re_ref):
    step = pl.program_id(1)
    nsteps = pl.num_programs(1)
    tm = ht_ref.shape[1]
    nloc = PEER_EB // PEER_NKEYS

    @pl.when(step == 0)
    def _():
        acc_ref[...] = jnp.zeros_like(acc_ref)
        re_ref[:, 0, :, 0:tm] = r1_ref[...]
        re_ref[:, 1, :, LANES:LANES + tm] = e1_ref[...]

    pt_ref[...] = _dot(u_ref[...].astype(BF16), ht_ref[...]).astype(BF16)

    pack = 2 * SUBLANES
    npk = PEER_NKEYS // pack
    zero = jnp.zeros((npk, pack, LANES), BF16)
    for mt in range(tm // LANES):
        ls = slice(mt * LANES, (mt + 1) * LANES)
        for iq in range(nloc // PEER_IG):
            gates = [zero] * PEER_IG
            for h in range(PEER_HEADS):
                r1 = re_ref[h, 0, :, ls].reshape(npk, pack, LANES)
                e1 = re_ref[h, 1, :, (mt + 1) * LANES:(mt + 2) * LANES].reshape(npk, pack, LANES)
                for k in range(PEER_IG):
                    il = iq * PEER_IG + k
                    nbv = jnp.broadcast_to(nb_ref[h, 0, il:il + 1, ls], (pack, LANES)).astype(BF16)
                    c0v = jnp.broadcast_to(c0_ref[h, 0, il:il + 1, ls], (pack, LANES)).astype(BF16)
                    gates[k] = gates[k] + jnp.where(r1 < nbv[None], e1, zero) * c0v[None]
            for k in range(PEER_IG):
                il = iq * PEER_IG + k
                rs = slice(il * PEER_NKEYS, (il + 1) * PEER_NKEYS)
                wt_ref[rs, ls] = gates[k].reshape(PEER_NKEYS, LANES) * _gelu(pt_ref[rs, ls])
    acc_ref[...] += _dot(vt_ref[0], wt_ref[...])

    @pl.when(step == nsteps - 1)
    def _():
        mod = mod_ref[0]
        g2 = mod[:, 5 * D_MODEL:6 * D_MODEL]
        ff = acc_ref[...].T
        out_ref[...] = _layer_norm(ALPHA * x1_ref[...] + g2 * ff, g_ref[...], b_ref[...])


def _peer_call(h2t, u_bf, vt_bf, route, x1, mod3, ln_g, ln_b, mod_row_fn):
    nb, c0, r1, e1 = route
    n = h2t.shape[1]
    nt = n // PEER_TM
    nblk = u_bf.shape[0] // PEER_EB
    big_spec = pl.BlockSpec((PEER_HEADS, PEER_NKEYS, PEER_TM), lambda t, i: (0, 0, t))
    nloc = PEER_EB // PEER_NKEYS
    nb = nb.reshape(PEER_HEADS, PEER_NKEYS // nloc, nloc, n)
    c0 = c0.reshape(PEER_HEADS, PEER_NKEYS // nloc, nloc, n)
    row_spec = pl.BlockSpec((PEER_HEADS, 1, nloc, PEER_TM), lambda t, i: (0, i, 0, t))
    return pl.pallas_call(
        _peer_kernel,
        grid=(nt, nblk),
        in_specs=[
            pl.BlockSpec((D_MODEL, PEER_TM), lambda t, i: (0, t)),
            pl.BlockSpec((PEER_EB, D_MODEL), lambda t, i: (i, 0)),
            pl.BlockSpec((1, D_MODEL, PEER_EB), lambda t, i: (i, 0, 0)),
            row_spec, row_spec, big_spec, big_spec,
            pl.BlockSpec((PEER_TM, D_MODEL), lambda t, i: (t, 0)),
            pl.BlockSpec((1, 1, 6 * D_MODEL), lambda t, i: (mod_row_fn(t), 0, 0)),
            pl.BlockSpec((1, D_MODEL), lambda t, i: (0, 0)),
            pl.BlockSpec((1, D_MODEL), lambda t, i: (0, 0)),
        ],
        out_specs=pl.BlockSpec((PEER_TM, D_MODEL), lambda t, i: (t, 0)),
        out_shape=jax.ShapeDtypeStruct((n, D_MODEL), F32),
        scratch_shapes=[
            pltpu.VMEM((PEER_EB, PEER_TM), BF16),
            pltpu.VMEM((PEER_EB, PEER_TM), BF16),
            pltpu.VMEM((D_MODEL, PEER_TM), F32),
            pltpu.VMEM((PEER_HEADS, 2, PEER_NKEYS, PEER_TM + LANES), BF16),
        ],
        compiler_params=_cparams(2, 56 << 20),
        name="peer",
    )(h2t, u_bf, vt_bf, nb, c0, r1, e1, x1, mod3, ln_g, ln_b)


def _block_diag(w):
    nh, bw, _ = w.shape
    eye = jnp.eye(nh, dtype=w.dtype)
    return (eye[:, None, :, None] * w[:, :, None, :]).reshape(nh * bw, nh * bw)


def _run_path(x, mod3, weights, s_hg0, s_rg0, seq, grid_rows, emit_state, mod_tok_row):
    bsz = x.shape[0]
    x2d = x.reshape(bsz * seq, D_MODEL)
    row256 = lambda t: mod_tok_row(t, TOK_TILE)
    prep = _inproj_call(x2d, mod3, weights["w_in"], weights["lb"], row256)
    qdf, kdf, kef, qdb, kdb, keb, v, vt, dec, sg, xr, ggr = prep
    og, st_hg = _hgrn_call((qdf, kdf, kef, qdb, kdb, keb, v, vt, dec, sg), weights["hg_norm"],
                           s_hg0, seq, emit_state)
    yrg, st_rg = _rglru_call(xr, ggr, weights["conv_w"], weights["conv_b"], weights["wg"],
                             weights["bg"], weights["lam"], s_rg0, seq, grid_rows, emit_state)
    x1, h2t = _oproj_call(og, yrg, x2d, mod3, weights["w_out"], weights["ln1_g"], weights["ln1_b"],
                          row256)
    route = _route_call(h2t, weights["wqt"], weights["keys"])
    out = _peer_call(h2t, weights["u"], weights["vt"], route, x1, mod3, weights["ln2_g"],
                     weights["ln2_b"], lambda t: mod_tok_row(t, PEER_TM))
    return out.reshape(bsz, seq, D_MODEL), st_hg, st_rg


def kernel(x_prompt, x_sample, c, state_hgrn, state_rglru, c_ctx, w_ada, b_ada, w_in, hgrn_lb,
           hgrn_norm_g, conv_w, conv_b, rg_wr, rg_br, rg_wi, rg_bi, rg_lam, w_out, ln1_g, ln1_b,
           peer_wq, peer_keys, peer_u, peer_v, ln2_g, ln2_b):
    assert w_ada.shape[0] == DEPTH
    bp, seq_p, _ = x_prompt.shape
    bs, seq_s, _ = x_sample.shape
    l = 0
    nrows = -(-(1 + bs) // SUBLANES) * SUBLANES
    cond = jnp.zeros((nrows, D_MODEL), F32).at[0].set(c_ctx).at[1:1 + bs].set(c)
    mod = _mod_call(cond, w_ada[l], b_ada[l][None, :])
    mod3 = mod[:, None, :]

    wg = jnp.concatenate([_block_diag(rg_wr[l, 0]), _block_diag(rg_wi[l, 0]),
                          _block_diag(rg_wr[l, 1]), _block_diag(rg_wi[l, 1])], axis=1)
    bg = jnp.concatenate([rg_br[l, 0], rg_bi[l, 0], rg_br[l, 1], rg_bi[l, 1]])[None, :]
    weights = {
        "w_in": w_in[l].astype(BF16),
        "lb": hgrn_lb,
        "hg_norm": hgrn_norm_g[l][None, :],
        "conv_w": conv_w[l],
        "conv_b": conv_b[l][None, :],
        "wg": wg.astype(BF16),
        "bg": bg,
        "lam": rg_lam[l],
        "w_out": w_out[l].astype(BF16),
        "ln1_g": ln1_g[l][None, :],
        "ln1_b": ln1_b[l][None, :],
        "wqt": peer_wq[l].T.astype(BF16),
        "keys": peer_keys[l].reshape(PEER_HEADS * 2, PEER_NKEYS, PEER_DQ // 2).astype(BF16),
        "u": peer_u[l],
        "vt": peer_v[l].reshape(-1, PEER_EB, D_MODEL).transpose(0, 2, 1).astype(BF16),
        "ln2_g": ln2_g[l][None, :],
        "ln2_b": ln2_b[l][None, :],
    }

    yp, st_hg, st_rg = _run_path(x_prompt, mod3, weights, None, None, seq_p, False, True,
                                 lambda t, tile: 0)
    ys, _, _ = _run_path(x_sample, mod3, weights, state_hgrn[:, l], state_rglru[:, l], seq_s, True,
                         False, lambda t, tile: 1 + (t * tile) // seq_s)
    new_hg = st_hg[:, None].astype(x_prompt.dtype)
    new_rg = st_rg[:, None].astype(x_prompt.dtype)
    return (yp, ys, new_hg, new_rg)
```

```python
import functools

import jax
import jax.numpy as jnp
from jax import lax
from jax.experimental import pallas as pl
from jax.experimental.pallas import tpu as pltpu

F32 = jnp.float32
BF16 = jnp.bfloat16

D_MODEL = 1024
HG_HEADS = 4
HG_DK = 128
HG_W = 512
RG_W = 512
RG_HEADS = 8
RG_BW = 64
RG_C = 8.0
CONV_W = 4
CONV_LEFT = CONV_W // 2
GRID_W = 64
CHUNK = 32
D_IN = 5 * HG_W + 2 * RG_W
PEER_HEADS = 8
PEER_NKEYS = 128
PEER_TOPK = 16
PEER_DQ = 256
DEPTH = 1
ALPHA = (2.0 * DEPTH) ** 0.25
LN_EPS = 1e-5
RMS_EPS = 1e-6

LANES = 128
SUBLANES = 8
VMEM_CAP_BYTES = 56 * 1024 * 1024

TOK_TILE = 256
GROUP = 128
GPS = SUBLANES * CHUNK // GROUP
PEER_TM = 512
PEER_EB = 2048
PEER_IG = 4
ROUTE_TM = 256


def _cparams(n_axes, vmem_bytes, flags=None):
    return pltpu.CompilerParams(
        dimension_semantics=("arbitrary",) * n_axes,
        vmem_limit_bytes=min(int(vmem_bytes), VMEM_CAP_BYTES),
        flags=flags,
    )


def _silu(x):
    return x * jax.nn.sigmoid(x)


def _gelu(x):
    return jax.nn.gelu(x, approximate=True)


def _dot(a, b):
    return jnp.dot(a, b, preferred_element_type=F32)


def _dot_nt(a, b):
    return lax.dot_general(a, b, (((1,), (1,)), ((), ())), preferred_element_type=F32)


def _layer_norm(x, g, b):
    mu = jnp.mean(x, -1, keepdims=True)
    xc = x - mu
    var = jnp.mean(xc * xc, -1, keepdims=True)
    return xc * lax.rsqrt(var + LN_EPS) * g + b


def _mod_kernel(cond_ref, w_ref, b_ref, o_ref):
    c = _silu(cond_ref[...]).astype(BF16)
    o_ref[...] = _dot(c, w_ref[...].astype(BF16)) + b_ref[...]


def _mod_call(cond, w_ada, b_ada):
    rows = cond.shape[0]
    nblk = w_ada.shape[1] // D_MODEL
    return pl.pallas_call(
        _mod_kernel,
        grid=(nblk,),
        in_specs=[
            pl.BlockSpec((rows, D_MODEL), lambda j: (0, 0)),
            pl.BlockSpec((D_MODEL, D_MODEL), lambda j: (0, j)),
            pl.BlockSpec((1, D_MODEL), lambda j: (0, j)),
        ],
        out_specs=pl.BlockSpec((rows, D_MODEL), lambda j: (0, j)),
        out_shape=jax.ShapeDtypeStruct((rows, w_ada.shape[1]), F32),
        compiler_params=_cparams(1, 24 << 20),
        name="mod",
    )(cond, w_ada, b_ada)


def _mask01(cond):
    return jnp.where(cond, 1.0, 0.0).astype(BF16)


def _split2(x):
    hi = x.astype(BF16)
    lo = (x - hi.astype(F32)).astype(BF16)
    return hi, lo


def _inproj_kernel(x_ref, mod_ref, w_ref, lb_ref,
                   qdf_ref, kdf_ref, kef_ref, qdb_ref, kdb_ref, keb_ref,
                   v_ref, vt_ref, dec_ref, sg_ref, xr_ref, ggr_ref):
    tt = x_ref.shape[0]
    mod = mod_ref[0]
    sh1 = mod[:, 0:D_MODEL]
    sc1 = mod[:, D_MODEL:2 * D_MODEL]
    h = (x_ref[...] * (1.0 + sc1) + sh1).astype(BF16)
    z = _dot(h, w_ref[...])
    q = z[:, 0:HG_W]
    iv = z[:, HG_W:2 * HG_W]
    g = z[:, 4 * HG_W:5 * HG_W]
    xr = z[:, 5 * HG_W:5 * HG_W + RG_W]
    gr = z[:, 5 * HG_W + RG_W:]
    qs = _silu(q)
    ivb = iv.astype(BF16)
    v_ref[...] = ivb
    ivt = iv.T.astype(BF16)
    for gi in range(tt // GROUP):
        vt_ref[gi] = ivt[:, gi * GROUP:(gi + 1) * GROUP]
    sg_ref[...] = _silu(g)
    xr_ref[...] = xr
    ggr_ref[...] = _gelu(gr)

    lbp = lb_ref[...]
    mx = jnp.max(lbp, axis=1, keepdims=True)
    e = jnp.exp(lbp - mx)
    lb_all = e[:, 0, :] / jnp.sum(e, axis=1)

    row = lax.broadcasted_iota(jnp.int32, (tt, tt), 0)
    col = lax.broadcasted_iota(jnp.int32, (tt, tt), 1)
    same = (row // CHUNK) == (col // CHUNK)
    tot = _mask01(same)
    srow = lax.broadcasted_iota(jnp.int32, (tt // CHUNK, tt), 0)
    scol = lax.broadcasted_iota(jnp.int32, (tt // CHUNK, tt), 1)
    sel = _mask01(srow == scol // CHUNK)

    outs = ((qdf_ref, kdf_ref, kef_ref), (qdb_ref, kdb_ref, keb_ref))
    for d in range(2):
        zf = z[:, (2 + d) * HG_W:(3 + d) * HG_W]
        lb = lb_all[d:d + 1, :]
        sig = jax.nn.sigmoid(zf)
        f = lb + (1.0 - lb) * sig
        k = (1.0 - lb) * (1.0 - sig)
        lf = jnp.log(f)
        lf_hi, lf_lo = _split2(lf)
        tri = _mask01(same & ((col <= row) if d == 0 else (col >= row)))
        b = _dot(tri, lf_hi) + _dot(tri, lf_lo)
        btot = _dot(tot, lf_hi) + _dot(tot, lf_lo)
        qd_ref, kd_ref, ke_ref = outs[d]
        qd_ref[...] = (qs * jnp.exp(b)).astype(BF16)
        kd_ref[...] = (k * jnp.exp(-b)).astype(BF16)
        ke_ref[...] = (k * jnp.exp(btot - b)).astype(BF16)
        dec_ref[:, d * HG_W:(d + 1) * HG_W] = jnp.exp(_dot(sel, lf_hi) + _dot(sel, lf_lo))


def _inproj_call(x2d, mod3, w_in_bf, lb_p, mod_row_fn):
    n = x2d.shape[0]
    nt = n // TOK_TILE
    tok_bf = lambda: jax.ShapeDtypeStruct((n, HG_W), BF16)
    tok_f = lambda: jax.ShapeDtypeStruct((n, HG_W), F32)
    tok_spec = pl.BlockSpec((TOK_TILE, HG_W), lambda t: (t, 0))
    gpt = TOK_TILE // GROUP
    cpt = TOK_TILE // CHUNK
    return pl.pallas_call(
        _inproj_kernel,
        grid=(nt,),
        in_specs=[
            pl.BlockSpec((TOK_TILE, D_MODEL), lambda t: (t, 0)),
            pl.BlockSpec((1, 1, 6 * D_MODEL), lambda t: (mod_row_fn(t), 0, 0)),
            pl.BlockSpec((D_MODEL, D_IN), lambda t: (0, 0)),
            pl.BlockSpec(lb_p.shape, lambda t: (0, 0, 0)),
        ],
        out_specs=[tok_spec] * 7 + [
            pl.BlockSpec((gpt, HG_W, GROUP), lambda t: (t, 0, 0)),
            pl.BlockSpec((cpt, 2 * HG_W), lambda t: (t, 0)),
            tok_spec, tok_spec, tok_spec,
        ],
        out_shape=[tok_bf() for _ in range(7)] + [
            jax.ShapeDtypeStruct((n // GROUP, HG_W, GROUP), BF16),
            jax.ShapeDtypeStruct((n // CHUNK, 2 * HG_W), F32),
            tok_f(), tok_f(), tok_f(),
        ],
        compiler_params=_cparams(1, 48 << 20),
        name="inproj",
    )(x2d, mod3, w_in_bf, lb_p)


def _hgrn_kernel(*refs, seq, has_init, emit_state):
    (qdf_ref, kdf_ref, kef_ref, qdb_ref, kdb_ref, keb_ref, v_ref, vt_ref, dec_ref, sg_ref,
     ng_ref) = refs[:11]
    pos = 11
    s0_ref = None
    if has_init:
        s0_ref = refs[pos]
        pos += 1
    og_ref = refs[pos]
    pos += 1
    st_ref = None
    if emit_state:
        st_ref = refs[pos]
        pos += 1
    oacc_ref, stt_ref = refs[pos:pos + 2]

    ngroups = seq // GROUP
    cpg = GROUP // CHUNK
    assert GPS * cpg == SUBLANES and ngroups % GPS == 0
    nsteps = ngroups // GPS
    row = lax.broadcasted_iota(jnp.int32, (GROUP, GROUP), 0)
    col = lax.broadcasted_iota(jnp.int32, (GROUP, GROUP), 1)
    same = (row // CHUNK) == (col // CHUNK)
    dirs = ((qdf_ref, kdf_ref, kef_ref), (qdb_ref, kdb_ref, keb_ref))
    for d in range(2):
        for h in range(HG_HEADS):
            if has_init:
                stt_ref[d, h] = s0_ref[0, d, h].T
            else:
                stt_ref[d, h] = jnp.zeros((HG_DK, HG_DK), F32)

    def body(pi, carry):
        for d in range(2):
            qd_ref, kd_ref, ke_ref = dirs[d]
            keep = same & ((col <= row) if d == 0 else (col >= row))
            pr = pi if d == 0 else nsteps - 1 - pi
            r8 = pl.multiple_of(pr * SUBLANES, SUBLANES)
            for h in range(HG_HEADS):
                cs = slice(h * HG_DK, (h + 1) * HG_DK)
                dec8 = dec_ref[pl.ds(r8, SUBLANES), d * HG_W + h * HG_DK:d * HG_W + (h + 1) * HG_DK]
                st = stt_ref[d, h]
                for gg in range(GPS):
                    sub = gg if d == 0 else GPS - 1 - gg
                    g = pr * GPS + sub
                    r0 = pl.multiple_of(g * GROUP, GROUP)
                    qd = qd_ref[pl.ds(r0, GROUP), cs]
                    kd = kd_ref[pl.ds(r0, GROUP), cs]
                    ke = ke_ref[pl.ds(r0, GROUP), cs]
                    vg = v_ref[pl.ds(r0, GROUP), cs]
                    vtg = vt_ref[g, cs, :]
                    att = jnp.where(keep, _dot_nt(qd, kd), 0.0).astype(BF16)
                    o = _dot(att, vg)
                    vstack = jnp.concatenate(
                        [vtg * _mask01((col // CHUNK) == c) for c in range(cpg)], axis=0)
                    dsts = _dot(vstack, ke)
                    parts = [None] * cpg
                    for cc in range(cpg):
                        c = cc if d == 0 else cpg - 1 - cc
                        qc = qd[c * CHUNK:(c + 1) * CHUNK]
                        parts[c] = _dot_nt(qc, st.astype(BF16))
                        decay = dec8[sub * cpg + c:sub * cpg + c + 1, :]
                        st = decay * st + dsts[c * HG_DK:(c + 1) * HG_DK]
                    oacc_ref[d, pl.ds(r0, GROUP), cs] = o + jnp.concatenate(parts, axis=0)
                stt_ref[d, h] = st
        return carry

    lax.fori_loop(0, nsteps, body, 0)
    if emit_state:
        for d in range(2):
            for h in range(HG_HEADS):
                st_ref[0, d, h] = stt_ref[d, h].T

    for h in range(HG_HEADS):
        cs = slice(h * HG_DK, (h + 1) * HG_DK)
        o = oacc_ref[0, :, cs] + oacc_ref[1, :, cs]
        ms = jnp.mean(o * o, -1, keepdims=True)
        on = o * lax.rsqrt(ms + RMS_EPS) * ng_ref[:, cs]
        og_ref[:, cs] = (on * sg_ref[:, cs]).astype(BF16)


def _hgrn_call(prep, norm_g, s0, seq, emit_state):
    qdf, kdf, kef, qdb, kdb, keb, v, vt, dec, sg = prep
    n = v.shape[0]
    nb = n // seq
    has_init = s0 is not None
    tok_spec = pl.BlockSpec((seq, HG_W), lambda b: (b, 0))
    in_specs = [tok_spec] * 7 + [
        pl.BlockSpec((seq // GROUP, HG_W, GROUP), lambda b: (b, 0, 0)),
        pl.BlockSpec((seq // CHUNK, 2 * HG_W), lambda b: (b, 0)),
        tok_spec,
        pl.BlockSpec((1, HG_W), lambda b: (0, 0)),
    ]
    args = [qdf, kdf, kef, qdb, kdb, keb, v, vt, dec, sg, norm_g]
    st_block = (1, 2, HG_HEADS, HG_DK, HG_DK)
    if has_init:
        in_specs.append(pl.BlockSpec(st_block, lambda b: (b, 0, 0, 0, 0)))
        args.append(s0)
    out_specs = [tok_spec]
    out_shape = [jax.ShapeDtypeStruct((n, HG_W), BF16)]
    if emit_state:
        out_specs.append(pl.BlockSpec(st_block, lambda b: (b, 0, 0, 0, 0)))
        out_shape.append(jax.ShapeDtypeStruct((nb, 2, HG_HEADS, HG_DK, HG_DK), F32))
    res = pl.pallas_call(
        functools.partial(_hgrn_kernel, seq=seq, has_init=has_init, emit_state=emit_state),
        grid=(nb,),
        in_specs=in_specs,
        out_specs=out_specs,
        out_shape=out_shape,
        scratch_shapes=[pltpu.VMEM((2, seq, HG_W), F32),
                        pltpu.VMEM((2, HG_HEADS, HG_DK, HG_DK), F32)],
        compiler_params=_cparams(1, 16 * seq * HG_W * 4 + (8 << 20)),
        name="hgrn",
    )(*args)
    return res if emit_state else (res[0], None)


def _shift_rows(x, off, nrows, row):
    if off == 0:
        return x
    y = pltpu.roll(x, (-off) % nrows, axis=0)
    ok = (row + off >= 0) & (row + off < nrows)
    return jnp.where(ok, y, 0.0)


def _scan_rows(a, u, reverse):
    n = a.shape[0]
    row = lax.broadcasted_iota(jnp.int32, a.shape, 0)
    s = 1
    while s < n:
        if reverse:
            a_sh = pltpu.roll(a, n - s, axis=0)
            u_sh = pltpu.roll(u, n - s, axis=0)
            ok = row < n - s
        else:
            a_sh = pltpu.roll(a, s, axis=0)
            u_sh = pltpu.roll(u, s, axis=0)
            ok = row >= s
        u = jnp.where(ok, a * u_sh + u, u)
        a = jnp.where(ok, a * a_sh, a)
        s *= 2
    return a, u


def _rglru_kernel(*refs, seq, grid_rows, has_init, emit_state):
    xr_ref, ggr_ref, cw_ref, cb_ref, wg_ref, bg_ref, lam_ref = refs[:7]
    pos = 7
    h0_ref = None
    if has_init:
        h0_ref = refs[pos]
        pos += 1
    y_ref = refs[pos]
    pos += 1
    st_ref = None
    if emit_state:
        st_ref = refs[pos]
        pos += 1
    a_ref, u_ref, hsum_ref = refs[pos:pos + 3]

    stride = GRID_W if grid_rows else 1
    row = lax.broadcasted_iota(jnp.int32, (seq, RG_W), 0)
    x = xr_ref[...]
    xc = jnp.zeros_like(x) + cb_ref[...]
    for j in range(CONV_W):
        xc = xc + cw_ref[j:j + 1, :] * _shift_rows(x, (j - CONV_LEFT) * stride, seq, row)
    gates = jax.nn.sigmoid(_dot(xc.astype(BF16), wg_ref[...]) + bg_ref[...])

    lam = lam_ref[...]
    nl = -lam
    softplus = jnp.maximum(nl, 0.0) + jnp.log1p(jnp.exp(-jnp.abs(nl)))

    for d in range(2):
        r = gates[:, (2 * d) * RG_W:(2 * d + 1) * RG_W]
        i = gates[:, (2 * d + 1) * RG_W:(2 * d + 2) * RG_W]
        log_a = -RG_C * r * softplus[d:d + 1, :]
        a = jnp.exp(log_a)
        u = jnp.sqrt(jnp.tanh(-log_a) * (a * a + 1.0)) * (i * xc)
        if has_init:
            h0 = h0_ref[0, d:d + 1, :]
        else:
            h0 = jnp.zeros((1, RG_W), F32)
        rev = d == 1
        if not grid_rows:
            acum, hz = _scan_rows(a, u, rev)
            h = hz + acum * h0
            last = h[0:1, :] if rev else h[seq - 1:seq, :]
        else:
            nrow = seq // GRID_W
            a_ref[...] = a
            u_ref[...] = u
            order = range(nrow - 1, -1, -1) if rev else range(nrow)
            hl = jnp.zeros((GRID_W, RG_W), F32)
            ac = jnp.ones((GRID_W, RG_W), F32)
            for rr in order:
                sl = slice(rr * GRID_W, (rr + 1) * GRID_W)
                ar = a_ref[sl, :]
                hl = ar * hl + u_ref[sl, :]
                ac = ar * ac
                u_ref[sl, :] = hl
                a_ref[sl, :] = ac
            ccum, hend0 = _scan_rows(ac, hl, rev)
            hend = hend0 + ccum * h0
            crow = lax.broadcasted_iota(jnp.int32, (GRID_W, RG_W), 0)
            if rev:
                hin = jnp.where(crow == GRID_W - 1, h0, pltpu.roll(hend, GRID_W - 1, axis=0))
                last = hend[0:1, :]
            else:
                hin = jnp.where(crow == 0, h0, pltpu.roll(hend, 1, axis=0))
                last = hend[GRID_W - 1:GRID_W, :]
            for rr in range(nrow):
                sl = slice(rr * GRID_W, (rr + 1) * GRID_W)
                u_ref[sl, :] = u_ref[sl, :] + a_ref[sl, :] * hin
            h = u_ref[...]
        if d == 0:
            hsum_ref[...] = h
        else:
            hsum_ref[...] = hsum_ref[...] + h
        if emit_state:
            st_ref[0, d:d + 1, :] = last
    y_ref[...] = (hsum_ref[...] * ggr_ref[...]).astype(BF16)


def _rglru_call(xr, ggr, conv_w, conv_b, wg, bg, lam, h0, seq, grid_rows, emit_state):
    n = xr.shape[0]
    nb = n // seq
    has_init = h0 is not None
    tok_spec = pl.BlockSpec((seq, RG_W), lambda b: (b, 0))
    full = lambda a: pl.BlockSpec(a.shape, lambda b: (0,) * a.ndim)
    in_specs = [tok_spec, tok_spec, full(conv_w), full(conv_b), full(wg), full(bg), full(lam)]
    args = [xr, ggr, conv_w, conv_b, wg, bg, lam]
    if has_init:
        in_specs.append(pl.BlockSpec((1, 2, RG_W), lambda b: (b, 0, 0)))
        args.append(h0)
    out_specs = [tok_spec]
    out_shape = [jax.ShapeDtypeStruct((n, RG_W), BF16)]
    if emit_state:
        out_specs.append(pl.BlockSpec((1, 2, RG_W), lambda b: (b, 0, 0)))
        out_shape.append(jax.ShapeDtypeStruct((nb, 2, RG_W), F32))
    res = pl.pallas_call(
        functools.partial(_rglru_kernel, seq=seq, grid_rows=grid_rows, has_init=has_init,
                          emit_state=emit_state),
        grid=(nb,),
        in_specs=in_specs,
        out_specs=out_specs,
        out_shape=out_shape,
        scratch_shapes=[pltpu.VMEM((seq, RG_W), F32)] * 3,
        compiler_params=_cparams(1, 20 * seq * RG_W * 4 + (8 << 20)),
        name="rglru",
    )(*args)
    return res if emit_state else (res[0], None)


def _oproj_kernel(og_ref, yrg_ref, x_ref, mod_ref, wo_ref, g_ref, b_ref, x1_ref, h2t_ref):
    mod = mod_ref[0]
    g1 = mod[:, 2 * D_MODEL:3 * D_MODEL]
    sh2 = mod[:, 3 * D_MODEL:4 * D_MODEL]
    sc2 = mod[:, 4 * D_MODEL:5 * D_MODEL]
    mix = _dot(og_ref[...], wo_ref[0:HG_W, :]) + _dot(yrg_ref[...], wo_ref[HG_W:, :])
    x1 = _layer_norm(ALPHA * x_ref[...] + g1 * mix, g_ref[...], b_ref[...])
    x1_ref[...] = x1
    h2 = x1 * (1.0 + sc2) + sh2
    h2t_ref[...] = h2.T.astype(BF16)


def _oproj_call(og, yrg, x2d, mod3, w_out_bf, ln_g, ln_b, mod_row_fn):
    n = x2d.shape[0]
    nt = n // TOK_TILE
    return pl.pallas_call(
        _oproj_kernel,
        grid=(nt,),
        in_specs=[
            pl.BlockSpec((TOK_TILE, HG_W), lambda t: (t, 0)),
            pl.BlockSpec((TOK_TILE, RG_W), lambda t: (t, 0)),
            pl.BlockSpec((TOK_TILE, D_MODEL), lambda t: (t, 0)),
            pl.BlockSpec((1, 1, 6 * D_MODEL), lambda t: (mod_row_fn(t), 0, 0)),
            pl.BlockSpec((D_MODEL, D_MODEL), lambda t: (0, 0)),
            pl.BlockSpec((1, D_MODEL), lambda t: (0, 0)),
            pl.BlockSpec((1, D_MODEL), lambda t: (0, 0)),
        ],
        out_specs=[
            pl.BlockSpec((TOK_TILE, D_MODEL), lambda t: (t, 0)),
            pl.BlockSpec((D_MODEL, TOK_TILE), lambda t: (0, t)),
        ],
        out_shape=[
            jax.ShapeDtypeStruct((n, D_MODEL), F32),
            jax.ShapeDtypeStruct((D_MODEL, n), BF16),
        ],
        compiler_params=_cparams(1, 32 << 20),
        name="oproj",
    )(og, yrg, x2d, mod3, w_out_bf, ln_g, ln_b)


def _cmpx(vals, i, j):
    hi = jnp.maximum(vals[i], vals[j])
    lo = jnp.minimum(vals[i], vals[j])
    vals[i] = hi
    vals[j] = lo


def _bitonic_merge_desc(vals):
    n = len(vals)
    d = n // 2
    while d >= 1:
        for i in range(n):
            if (i & d) == 0:
                _cmpx(vals, i, i + d)
        d //= 2
    return vals


def _sort_desc(vals):
    n = len(vals)
    if n == 1:
        return vals
    top = _sort_desc(vals[:n // 2])
    bot = _sort_desc(vals[n // 2:])
    return _bitonic_merge_desc(top + bot[::-1])


def _merge_top(a, b):
    n = len(a)
    return _bitonic_merge_desc([jnp.maximum(a[i], b[n - 1 - i]) for i in range(n)])


def _merge_sublanes(vals):
    s = SUBLANES // 2
    while s >= 1:
        rolled = [pltpu.roll(v, s, axis=0) for v in vals]
        vals = _merge_top(vals, rolled)
        s //= 2
    return vals


def _top_values(s):
    nv = PEER_NKEYS // SUBLANES
    vals = [s[i * SUBLANES:(i + 1) * SUBLANES, :] for i in range(nv)]
    vals = _sort_desc(vals)
    return _merge_sublanes(vals)


def _count_prefix(pred, rows):
    assert len(rows) == 16
    t16 = pred(rows[15])
    t8 = pred(rows[7])
    t4 = pred(jnp.where(t8, rows[11], rows[3]))
    t2 = pred(jnp.where(t8, jnp.where(t4, rows[13], rows[9]), jnp.where(t4, rows[5], rows[1])))
    hi = jnp.where(t4, jnp.where(t2, rows[14], rows[12]), jnp.where(t2, rows[10], rows[8]))
    lo = jnp.where(t4, jnp.where(t2, rows[6], rows[4]), jnp.where(t2, rows[2], rows[0]))
    t1 = pred(jnp.where(t8, hi, lo))
    cnt = (jnp.where(t8, 8.0, 0.0) + jnp.where(t4, 4.0, 0.0)
           + jnp.where(t2, 2.0, 0.0) + jnp.where(t1, 1.0, 0.0))
    return jnp.where(t16, 16.0, cnt)


def _route_kernel(ht_ref, wqt_ref, keys_ref, nb_ref, c0_ref, r1_ref, e1_ref):
    tm = ht_ref.shape[1]
    qt = _dot(wqt_ref[...], ht_ref[...]).astype(BF16)
    half = PEER_DQ // 2
    sub = lax.broadcasted_iota(jnp.int32, (SUBLANES, tm), 0)
    for h in range(PEER_HEADS):
        s = []
        top = []
        for p in range(2):
            r0 = (2 * h + p) * half
            sp = _dot(keys_ref[2 * h + p], qt[r0:r0 + half, :])
            s.append(sp)
            top.append(_top_values(sp))
        lists = []
        for gb in range(PEER_TOPK // SUBLANES):
            bsel = jnp.zeros((SUBLANES, tm), F32)
            for b in range(SUBLANES):
                bsel = jnp.where(sub == b, top[1][gb * SUBLANES + b], bsel)
            lists.append([top[0][a] + bsel for a in range(PEER_TOPK)])
        cand = lists[0]
        for gb in range(1, len(lists)):
            cand = _merge_top(cand, lists[gb])
        fv = _merge_sublanes(cand)
        mx = fv[0][0:1, :]
        zsum = jnp.zeros((1, tm), F32)
        for kk in range(PEER_TOPK):
            zsum = zsum + jnp.exp(fv[kk][0:1, :] - mx)
        tau = fv[PEER_TOPK - 1][0:1, :]
        sv1 = [top[1][b][0:1, :] for b in range(PEER_TOPK)]
        rb = 4 * SUBLANES
        for r0 in range(0, PEER_NKEYS, rb):
            s0b = s[0][r0:r0 + rb, :]
            s1b = s[1][r0:r0 + rb, :]
            nb_ref[h, r0:r0 + rb, :] = _count_prefix(lambda r: s0b + r >= tau, sv1)
            r1_ref[h, r0:r0 + rb, :] = _count_prefix(lambda r: r > s1b, sv1).astype(BF16)
        e1_ref[h] = jnp.exp(s[1] - top[1][0][0:1, :]).astype(BF16)
        c0_ref[h] = jnp.exp(s[0] - top[0][0][0:1, :]) / zsum


def _route_call(h2t, wqt_bf, keys_bf):
    n = h2t.shape[1]
    nt = n // ROUTE_TM
    big = lambda dt: jax.ShapeDtypeStruct((PEER_HEADS, PEER_NKEYS, n), dt)
    big_spec = pl.BlockSpec((PEER_HEADS, PEER_NKEYS, ROUTE_TM), lambda t: (0, 0, t))
    return pl.pallas_call(
        _route_kernel,
        grid=(nt,),
        in_specs=[
            pl.BlockSpec((D_MODEL, ROUTE_TM), lambda t: (0, t)),
            pl.BlockSpec(wqt_bf.shape, lambda t: (0, 0)),
            pl.BlockSpec(keys_bf.shape, lambda t: (0, 0, 0)),
        ],
        out_specs=[big_spec] * 4,
        out_shape=[big(F32), big(F32), big(BF16), big(BF16)],
        compiler_params=_cparams(1, 40 << 20),
        name="route",
    )(h2t, wqt_bf, keys_bf)


def _peer_kernel(ht_ref, u_ref, vt_ref, nb_ref, c0_ref, r1_ref, e1_ref,
                 x1_ref, mod_ref, g_ref, b_ref, out_ref, pt_ref, wt_ref, acc_ref, re_ref):
    step = pl.program_id(1)
    nsteps = pl.num_programs(1)
    tm = ht_ref.shape[1]
    nloc = PEER_EB // PEER_NKEYS

    @pl.when(step == 0)
    def _():
        acc_ref[...] = jnp.zeros_like(acc_ref)
        re_ref[:, 0, :, 0:tm] = r1_ref[...]
        re_ref[:, 1, :, LANES:LANES + tm] = e1_ref[...]

    pt_ref[...] = _dot(u_ref[...].astype(BF16), ht_ref[...]).astype(BF16)

    pack = 2 * SUBLANES
    npk = PEER_NKEYS // pack
    zero = jnp.zeros((npk, pack, LANES), BF16)
    for mt in range(tm // LANES):
        ls = slice(mt * LANES, (mt + 1) * LANES)
        for iq in range(nloc // PEER_IG):
            gates = [None] * PEER_IG
            for h in range(PEER_HEADS):
                r1 = re_ref[h, 0, :, ls].reshape(npk, pack, LANES)
                e1 = re_ref[h, 1, :, (mt + 1) * LANES:(mt + 2) * LANES].reshape(npk, pack, LANES)
                for k in range(PEER_IG):
                    il = iq * PEER_IG + k
                    nbv = jnp.broadcast_to(nb_ref[h, 0, il:il + 1, ls], (pack, LANES)).astype(BF16)
                    c0v = jnp.broadcast_to(c0_ref[h, 0, il:il + 1, ls], (pack, LANES)).astype(BF16)
                    term = jnp.where(r1 < nbv[None], e1, zero) * c0v[None]
                    gates[k] = term if h == 0 else gates[k] + term
            for k in range(PEER_IG):
                il = iq * PEER_IG + k
                rs = slice(il * PEER_NKEYS, (il + 1) * PEER_NKEYS)
                wt_ref[rs, ls] = gates[k].reshape(PEER_NKEYS, LANES) * _gelu(pt_ref[rs, ls])
    acc_ref[...] += _dot(vt_ref[0], wt_ref[...])

    @pl.when(step == nsteps - 1)
    def _():
        mod = mod_ref[0]
        g2 = mod[:, 5 * D_MODEL:6 * D_MODEL]
        ff = acc_ref[...].T
        out_ref[...] = _layer_norm(ALPHA * x1_ref[...] + g2 * ff, g_ref[...], b_ref[...])


def _peer_call(h2t, u_bf, vt_bf, route, x1, mod3, ln_g, ln_b, mod_row_fn):
    nb, c0, r1, e1 = route
    n = h2t.shape[1]
    nt = n // PEER_TM
    nblk = u_bf.shape[0] // PEER_EB
    big_spec = pl.BlockSpec((PEER_HEADS, PEER_NKEYS, PEER_TM), lambda t, i: (0, 0, t))
    nloc = PEER_EB // PEER_NKEYS
    nb = nb.reshape(PEER_HEADS, PEER_NKEYS // nloc, nloc, n)
    c0 = c0.reshape(PEER_HEADS, PEER_NKEYS // nloc, nloc, n)
    row_spec = pl.BlockSpec((PEER_HEADS, 1, nloc, PEER_TM), lambda t, i: (0, i, 0, t))
    return pl.pallas_call(
        _peer_kernel,
        grid=(nt, nblk),
        in_specs=[
            pl.BlockSpec((D_MODEL, PEER_TM), lambda t, i: (0, t)),
            pl.BlockSpec((PEER_EB, D_MODEL), lambda t, i: (i, 0)),
            pl.BlockSpec((1, D_MODEL, PEER_EB), lambda t, i: (i, 0, 0)),
            row_spec, row_spec, big_spec, big_spec,
            pl.BlockSpec((PEER_TM, D_MODEL), lambda t, i: (t, 0)),
            pl.BlockSpec((1, 1, 6 * D_MODEL), lambda t, i: (mod_row_fn(t), 0, 0)),
            pl.BlockSpec((1, D_MODEL), lambda t, i: (0, 0)),
            pl.BlockSpec((1, D_MODEL), lambda t, i: (0, 0)),
        ],
        out_specs=pl.BlockSpec((PEER_TM, D_MODEL), lambda t, i: (t, 0)),
        out_shape=jax.ShapeDtypeStruct((n, D_MODEL), F32),
        scratch_shapes=[
            pltpu.VMEM((PEER_EB, PEER_TM), BF16),
            pltpu.VMEM((PEER_EB, PEER_TM), BF16),
            pltpu.VMEM((D_MODEL, PEER_TM), F32),
            pltpu.VMEM((PEER_HEADS, 2, PEER_NKEYS, PEER_TM + LANES), BF16),
        ],
        compiler_params=_cparams(2, 56 << 20),
        name="peer",
    )(h2t, u_bf, vt_bf, nb, c0, r1, e1, x1, mod3, ln_g, ln_b)


def _block_diag(w):
    nh, bw, _ = w.shape
    eye = jnp.eye(nh, dtype=w.dtype)
    return (eye[:, None, :, None] * w[:, :, None, :]).reshape(nh * bw, nh * bw)


def _run_path(x, mod3, weights, s_hg0, s_rg0, seq, grid_rows, emit_state, mod_tok_row):
    bsz = x.shape[0]
    x2d = x.reshape(bsz * seq, D_MODEL)
    row256 = lambda t: mod_tok_row(t, TOK_TILE)
    prep = _inproj_call(x2d, mod3, weights["w_in"], weights["lb"], row256)
    qdf, kdf, kef, qdb, kdb, keb, v, vt, dec, sg, xr, ggr = prep
    og, st_hg = _hgrn_call((qdf, kdf, kef, qdb, kdb, keb, v, vt, dec, sg), weights["hg_norm"],
                           s_hg0, seq, emit_state)
    yrg, st_rg = _rglru_call(xr, ggr, weights["conv_w"], weights["conv_b"], weights["wg"],
                             weights["bg"], weights["lam"], s_rg0, seq, grid_rows, emit_state)
    x1, h2t = _oproj_call(og, yrg, x2d, mod3, weights["w_out"], weights["ln1_g"], weights["ln1_b"],
                          row256)
    route = _route_call(h2t, weights["wqt"], weights["keys"])
    out = _peer_call(h2t, weights["u"], weights["vt"], route, x1, mod3, weights["ln2_g"],
                     weights["ln2_b"], lambda t: mod_tok_row(t, PEER_TM))
    return out.reshape(bsz, seq, D_MODEL), st_hg, st_rg


def kernel(x_prompt, x_sample, c, state_hgrn, state_rglru, c_ctx, w_ada, b_ada, w_in, hgrn_lb,
           hgrn_norm_g, conv_w, conv_b, rg_wr, rg_br, rg_wi, rg_bi, rg_lam, w_out, ln1_g, ln1_b,
           peer_wq, peer_keys, peer_u, peer_v, ln2_g, ln2_b):
    assert w_ada.shape[0] == DEPTH
    bp, seq_p, _ = x_prompt.shape
    bs, seq_s, _ = x_sample.shape
    l = 0
    nrows = -(-(1 + bs) // SUBLANES) * SUBLANES
    cond = jnp.zeros((nrows, D_MODEL), F32).at[0].set(c_ctx).at[1:1 + bs].set(c)
    mod = _mod_call(cond, w_ada[l], b_ada[l][None, :])
    mod3 = mod[:, None, :]

    wg = jnp.concatenate([_block_diag(rg_wr[l, 0]), _block_diag(rg_wi[l, 0]),
                          _block_diag(rg_wr[l, 1]), _block_diag(rg_wi[l, 1])], axis=1)
    bg = jnp.concatenate([rg_br[l, 0], rg_bi[l, 0], rg_br[l, 1], rg_bi[l, 1]])[None, :]
    weights = {
        "w_in": w_in[l].astype(BF16),
        "lb": hgrn_lb,
        "hg_norm": hgrn_norm_g[l][None, :],
        "conv_w": conv_w[l],
        "conv_b": conv_b[l][None, :],
        "wg": wg.astype(BF16),
        "bg": bg,
        "lam": rg_lam[l],
        "w_out": w_out[l].astype(BF16),
        "ln1_g": ln1_g[l][None, :],
        "ln1_b": ln1_b[l][None, :],
        "wqt": peer_wq[l].T.astype(BF16),
        "keys": peer_keys[l].reshape(PEER_HEADS * 2, PEER_NKEYS, PEER_DQ // 2).astype(BF16),
        "u": peer_u[l],
        "vt": peer_v[l].reshape(-1, PEER_EB, D_MODEL).transpose(0, 2, 1).astype(BF16),
        "ln2_g": ln2_g[l][None, :],
        "ln2_b": ln2_b[l][None, :],
    }

    yp, st_hg, st_rg = _run_path(x_prompt, mod3, weights, None, None, seq_p, False, True,
                                 lambda t, tile: 0)
    ys, _, _ = _run_path(x_sample, mod3, weights, state_hgrn[:, l], state_rglru[:, l], seq_s, True,
                         False, lambda t, tile: 1 + (t * tile) // seq_s)
    new_hg = st_hg[:, None].astype(x_prompt.dtype)
    new_rg = st_rg[:, None].astype(x_prompt.dtype)
    return (yp, ys, new_hg, new_rg)
```

```python
import functools

import jax
import jax.numpy as jnp
from jax import lax
from jax.experimental import pallas as pl
from jax.experimental.pallas import tpu as pltpu

F32 = jnp.float32
BF16 = jnp.bfloat16

D_MODEL = 1024
HG_HEADS = 4
HG_DK = 128
HG_W = 512
RG_W = 512
RG_HEADS = 8
RG_BW = 64
RG_C = 8.0
CONV_W = 4
CONV_LEFT = CONV_W // 2
GRID_W = 64
CHUNK = 32
D_IN = 5 * HG_W + 2 * RG_W
PEER_HEADS = 8
PEER_NKEYS = 128
PEER_TOPK = 16
PEER_DQ = 256
DEPTH = 1
ALPHA = (2.0 * DEPTH) ** 0.25
LN_EPS = 1e-5
RMS_EPS = 1e-6

LANES = 128
SUBLANES = 8
VMEM_CAP_BYTES = 56 * 1024 * 1024

TOK_TILE = 256
GROUP = 128
GPS = SUBLANES * CHUNK // GROUP
PEER_TM = 512
PEER_EB = 2048
PEER_IG = 4
ROUTE_TM = 256


def _cparams(n_axes, vmem_bytes, flags=None):
    return pltpu.CompilerParams(
        dimension_semantics=("arbitrary",) * n_axes,
        vmem_limit_bytes=min(int(vmem_bytes), VMEM_CAP_BYTES),
        flags=flags,
    )


def _silu(x):
    return x * jax.nn.sigmoid(x)


def _gelu(x):
    return jax.nn.gelu(x, approximate=True)


def _dot(a, b):
    return jnp.dot(a, b, preferred_element_type=F32)


def _dot_nt(a, b):
    return lax.dot_general(a, b, (((1,), (1,)), ((), ())), preferred_element_type=F32)


def _layer_norm(x, g, b):
    mu = jnp.mean(x, -1, keepdims=True)
    xc = x - mu
    var = jnp.mean(xc * xc, -1, keepdims=True)
    return xc * lax.rsqrt(var + LN_EPS) * g + b


def _mod_kernel(cond_ref, w_ref, b_ref, o_ref):
    c = _silu(cond_ref[...]).astype(BF16)
    o_ref[...] = _dot(c, w_ref[...].astype(BF16)) + b_ref[...]


def _mod_call(cond, w_ada, b_ada):
    rows = cond.shape[0]
    nblk = w_ada.shape[1] // D_MODEL
    return pl.pallas_call(
        _mod_kernel,
        grid=(nblk,),
        in_specs=[
            pl.BlockSpec((rows, D_MODEL), lambda j: (0, 0)),
            pl.BlockSpec((D_MODEL, D_MODEL), lambda j: (0, j)),
            pl.BlockSpec((1, D_MODEL), lambda j: (0, j)),
        ],
        out_specs=pl.BlockSpec((rows, D_MODEL), lambda j: (0, j)),
        out_shape=jax.ShapeDtypeStruct((rows, w_ada.shape[1]), F32),
        compiler_params=_cparams(1, 24 << 20),
        name="mod",
    )(cond, w_ada, b_ada)


def _mask01(cond):
    return jnp.where(cond, 1.0, 0.0).astype(BF16)


def _split2(x):
    hi = x.astype(BF16)
    lo = (x - hi.astype(F32)).astype(BF16)
    return hi, lo


def _inproj_kernel(x_ref, mod_ref, w_ref, lb_ref,
                   qdf_ref, kdf_ref, kef_ref, qdb_ref, kdb_ref, keb_ref,
                   v_ref, vt_ref, dec_ref, sg_ref, xr_ref, ggr_ref):
    tt = x_ref.shape[0]
    mod = mod_ref[0]
    sh1 = mod[:, 0:D_MODEL]
    sc1 = mod[:, D_MODEL:2 * D_MODEL]
    h = (x_ref[...] * (1.0 + sc1) + sh1).astype(BF16)
    z = _dot(h, w_ref[...])
    q = z[:, 0:HG_W]
    iv = z[:, HG_W:2 * HG_W]
    g = z[:, 4 * HG_W:5 * HG_W]
    xr = z[:, 5 * HG_W:5 * HG_W + RG_W]
    gr = z[:, 5 * HG_W + RG_W:]
    qs = _silu(q)
    ivb = iv.astype(BF16)
    v_ref[...] = ivb
    ivt = iv.T.astype(BF16)
    for gi in range(tt // GROUP):
        vt_ref[gi] = ivt[:, gi * GROUP:(gi + 1) * GROUP]
    sg_ref[...] = _silu(g)
    xr_ref[...] = xr
    ggr_ref[...] = _gelu(gr)

    lbp = lb_ref[...]
    mx = jnp.max(lbp, axis=1, keepdims=True)
    e = jnp.exp(lbp - mx)
    lb_all = e[:, 0, :] / jnp.sum(e, axis=1)

    row = lax.broadcasted_iota(jnp.int32, (tt, tt), 0)
    col = lax.broadcasted_iota(jnp.int32, (tt, tt), 1)
    same = (row // CHUNK) == (col // CHUNK)
    tot = _mask01(same)
    srow = lax.broadcasted_iota(jnp.int32, (tt // CHUNK, tt), 0)
    scol = lax.broadcasted_iota(jnp.int32, (tt // CHUNK, tt), 1)
    sel = _mask01(srow == scol // CHUNK)

    outs = ((qdf_ref, kdf_ref, kef_ref), (qdb_ref, kdb_ref, keb_ref))
    for d in range(2):
        zf = z[:, (2 + d) * HG_W:(3 + d) * HG_W]
        lb = lb_all[d:d + 1, :]
        sig = jax.nn.sigmoid(zf)
        f = lb + (1.0 - lb) * sig
        k = (1.0 - lb) * (1.0 - sig)
        lf = jnp.log(f)
        lf_hi, lf_lo = _split2(lf)
        tri = _mask01(same & ((col <= row) if d == 0 else (col >= row)))
        b = _dot(tri, lf_hi) + _dot(tri, lf_lo)
        btot = _dot(tot, lf_hi) + _dot(tot, lf_lo)
        qd_ref, kd_ref, ke_ref = outs[d]
        qd_ref[...] = (qs * jnp.exp(b)).astype(BF16)
        kd_ref[...] = (k * jnp.exp(-b)).astype(BF16)
        ke_ref[...] = (k * jnp.exp(btot - b)).astype(BF16)
        dec_ref[:, d * HG_W:(d + 1) * HG_W] = jnp.exp(_dot(sel, lf_hi) + _dot(sel, lf_lo))


def _inproj_call(x2d, mod3, w_in_bf, lb_p, mod_row_fn):
    n = x2d.shape[0]
    nt = n // TOK_TILE
    tok_bf = lambda: jax.ShapeDtypeStruct((n, HG_W), BF16)
    tok_f = lambda: jax.ShapeDtypeStruct((n, HG_W), F32)
    tok_spec = pl.BlockSpec((TOK_TILE, HG_W), lambda t: (t, 0))
    gpt = TOK_TILE // GROUP
    cpt = TOK_TILE // CHUNK
    return pl.pallas_call(
        _inproj_kernel,
        grid=(nt,),
        in_specs=[
            pl.BlockSpec((TOK_TILE, D_MODEL), lambda t: (t, 0)),
            pl.BlockSpec((1, 1, 6 * D_MODEL), lambda t: (mod_row_fn(t), 0, 0)),
            pl.BlockSpec((D_MODEL, D_IN), lambda t: (0, 0)),
            pl.BlockSpec(lb_p.shape, lambda t: (0, 0, 0)),
        ],
        out_specs=[tok_spec] * 7 + [
            pl.BlockSpec((gpt, HG_W, GROUP), lambda t: (t, 0, 0)),
            pl.BlockSpec((cpt, 2 * HG_W), lambda t: (t, 0)),
            tok_spec, tok_spec, tok_spec,
        ],
        out_shape=[tok_bf() for _ in range(7)] + [
            jax.ShapeDtypeStruct((n // GROUP, HG_W, GROUP), BF16),
            jax.ShapeDtypeStruct((n // CHUNK, 2 * HG_W), F32),
            tok_f(), tok_f(), tok_f(),
        ],
        compiler_params=_cparams(1, 48 << 20),
        name="inproj",
    )(x2d, mod3, w_in_bf, lb_p)


def _hgrn_kernel(*refs, seq, has_init, emit_state):
    (qdf_ref, kdf_ref, kef_ref, qdb_ref, kdb_ref, keb_ref, v_ref, vt_ref, dec_ref, sg_ref,
     ng_ref) = refs[:11]
    pos = 11
    s0_ref = None
    if has_init:
        s0_ref = refs[pos]
        pos += 1
    og_ref = refs[pos]
    pos += 1
    st_ref = None
    if emit_state:
        st_ref = refs[pos]
        pos += 1
    oacc_ref, stt_ref = refs[pos:pos + 2]

    ngroups = seq // GROUP
    cpg = GROUP // CHUNK
    assert GPS * cpg == SUBLANES and ngroups % GPS == 0
    nsteps = ngroups // GPS
    row = lax.broadcasted_iota(jnp.int32, (GROUP, GROUP), 0)
    col = lax.broadcasted_iota(jnp.int32, (GROUP, GROUP), 1)
    same = (row // CHUNK) == (col // CHUNK)
    dirs = ((qdf_ref, kdf_ref, kef_ref), (qdb_ref, kdb_ref, keb_ref))
    for d in range(2):
        for h in range(HG_HEADS):
            if has_init:
                stt_ref[d, h] = s0_ref[0, d, h].T
            else:
                stt_ref[d, h] = jnp.zeros((HG_DK, HG_DK), F32)

    def body(pi, carry):
        for d in range(2):
            qd_ref, kd_ref, ke_ref = dirs[d]
            keep = same & ((col <= row) if d == 0 else (col >= row))
            pr = pi if d == 0 else nsteps - 1 - pi
            r8 = pl.multiple_of(pr * SUBLANES, SUBLANES)
            for h in range(HG_HEADS):
                cs = slice(h * HG_DK, (h + 1) * HG_DK)
                dec8 = dec_ref[pl.ds(r8, SUBLANES), d * HG_W + h * HG_DK:d * HG_W + (h + 1) * HG_DK]
                st = stt_ref[d, h]
                for gg in range(GPS):
                    sub = gg if d == 0 else GPS - 1 - gg
                    g = pr * GPS + sub
                    r0 = pl.multiple_of(g * GROUP, GROUP)
                    qd = qd_ref[pl.ds(r0, GROUP), cs]
                    kd = kd_ref[pl.ds(r0, GROUP), cs]
                    ke = ke_ref[pl.ds(r0, GROUP), cs]
                    vg = v_ref[pl.ds(r0, GROUP), cs]
                    vtg = vt_ref[g, cs, :]
                    att = jnp.where(keep, _dot_nt(qd, kd), 0.0).astype(BF16)
                    o = _dot(att, vg)
                    vstack = jnp.concatenate(
                        [vtg * _mask01((col // CHUNK) == c) for c in range(cpg)], axis=0)
                    dsts = _dot(vstack, ke)
                    parts = [None] * cpg
                    for cc in range(cpg):
                        c = cc if d == 0 else cpg - 1 - cc
                        qc = qd[c * CHUNK:(c + 1) * CHUNK]
                        parts[c] = _dot_nt(qc, st.astype(BF16))
                        decay = dec8[sub * cpg + c:sub * cpg + c + 1, :]
                        st = decay * st + dsts[c * HG_DK:(c + 1) * HG_DK]
                    oacc_ref[d, pl.ds(r0, GROUP), cs] = o + jnp.concatenate(parts, axis=0)
                stt_ref[d, h] = st
        return carry

    lax.fori_loop(0, nsteps, body, 0)
    if emit_state:
        for d in range(2):
            for h in range(HG_HEADS):
                st_ref[0, d, h] = stt_ref[d, h].T

    for h in range(HG_HEADS):
        cs = slice(h * HG_DK, (h + 1) * HG_DK)
        o = oacc_ref[0, :, cs] + oacc_ref[1, :, cs]
        ms = jnp.mean(o * o, -1, keepdims=True)
        on = o * lax.rsqrt(ms + RMS_EPS) * ng_ref[:, cs]
        og_ref[:, cs] = (on * sg_ref[:, cs]).astype(BF16)


def _hgrn_call(prep, norm_g, s0, seq, emit_state):
    qdf, kdf, kef, qdb, kdb, keb, v, vt, dec, sg = prep
    n = v.shape[0]
    nb = n // seq
    has_init = s0 is not None
    tok_spec = pl.BlockSpec((seq, HG_W), lambda b: (b, 0))
    in_specs = [tok_spec] * 7 + [
        pl.BlockSpec((seq // GROUP, HG_W, GROUP), lambda b: (b, 0, 0)),
        pl.BlockSpec((seq // CHUNK, 2 * HG_W), lambda b: (b, 0)),
        tok_spec,
        pl.BlockSpec((1, HG_W), lambda b: (0, 0)),
    ]
    args = [qdf, kdf, kef, qdb, kdb, keb, v, vt, dec, sg, norm_g]
    st_block = (1, 2, HG_HEADS, HG_DK, HG_DK)
    if has_init:
        in_specs.append(pl.BlockSpec(st_block, lambda b: (b, 0, 0, 0, 0)))
        args.append(s0)
    out_specs = [tok_spec]
    out_shape = [jax.ShapeDtypeStruct((n, HG_W), BF16)]
    if emit_state:
        out_specs.append(pl.BlockSpec(st_block, lambda b: (b, 0, 0, 0, 0)))
        out_shape.append(jax.ShapeDtypeStruct((nb, 2, HG_HEADS, HG_DK, HG_DK), F32))
    res = pl.pallas_call(
        functools.partial(_hgrn_kernel, seq=seq, has_init=has_init, emit_state=emit_state),
        grid=(nb,),
        in_specs=in_specs,
        out_specs=out_specs,
        out_shape=out_shape,
        scratch_shapes=[pltpu.VMEM((2, seq, HG_W), F32),
                        pltpu.VMEM((2, HG_HEADS, HG_DK, HG_DK), F32)],
        compiler_params=_cparams(1, 16 * seq * HG_W * 4 + (8 << 20)),
        name="hgrn",
    )(*args)
    return res if emit_state else (res[0], None)


def _shift_rows(x, off, nrows, row):
    if off == 0:
        return x
    y = pltpu.roll(x, (-off) % nrows, axis=0)
    ok = (row + off >= 0) & (row + off < nrows)
    return jnp.where(ok, y, 0.0)


def _scan_rows(a, u, reverse):
    n = a.shape[0]
    row = lax.broadcasted_iota(jnp.int32, a.shape, 0)
    s = 1
    while s < n:
        if reverse:
            a_sh = pltpu.roll(a, n - s, axis=0)
            u_sh = pltpu.roll(u, n - s, axis=0)
            ok = row < n - s
        else:
            a_sh = pltpu.roll(a, s, axis=0)
            u_sh = pltpu.roll(u, s, axis=0)
            ok = row >= s
        u = jnp.where(ok, a * u_sh + u, u)
        a = jnp.where(ok, a * a_sh, a)
        s *= 2
    return a, u


def _rglru_kernel(*refs, seq, grid_rows, has_init, emit_state):
    xr_ref, ggr_ref, cw_ref, cb_ref, wg_ref, bg_ref, lam_ref = refs[:7]
    pos = 7
    h0_ref = None
    if has_init:
        h0_ref = refs[pos]
        pos += 1
    y_ref = refs[pos]
    pos += 1
    st_ref = None
    if emit_state:
        st_ref = refs[pos]
        pos += 1
    a_ref, u_ref, hsum_ref = refs[pos:pos + 3]

    stride = GRID_W if grid_rows else 1
    row = lax.broadcasted_iota(jnp.int32, (seq, RG_W), 0)
    x = xr_ref[...]
    xc = jnp.zeros_like(x) + cb_ref[...]
    for j in range(CONV_W):
        xc = xc + cw_ref[j:j + 1, :] * _shift_rows(x, (j - CONV_LEFT) * stride, seq, row)
    gates = jax.nn.sigmoid(_dot(xc.astype(BF16), wg_ref[...]) + bg_ref[...])

    lam = lam_ref[...]
    nl = -lam
    softplus = jnp.maximum(nl, 0.0) + jnp.log1p(jnp.exp(-jnp.abs(nl)))

    for d in range(2):
        r = gates[:, (2 * d) * RG_W:(2 * d + 1) * RG_W]
        i = gates[:, (2 * d + 1) * RG_W:(2 * d + 2) * RG_W]
        log_a = -RG_C * r * softplus[d:d + 1, :]
        a = jnp.exp(log_a)
        u = jnp.sqrt(jnp.tanh(-log_a) * (a * a + 1.0)) * (i * xc)
        if has_init:
            h0 = h0_ref[0, d:d + 1, :]
        else:
            h0 = jnp.zeros((1, RG_W), F32)
        rev = d == 1
        if not grid_rows:
            acum, hz = _scan_rows(a, u, rev)
            h = hz + acum * h0
            last = h[0:1, :] if rev else h[seq - 1:seq, :]
        else:
            nrow = seq // GRID_W
            a_ref[...] = a
            u_ref[...] = u
            order = range(nrow - 1, -1, -1) if rev else range(nrow)
            hl = jnp.zeros((GRID_W, RG_W), F32)
            ac = jnp.ones((GRID_W, RG_W), F32)
            for rr in order:
                sl = slice(rr * GRID_W, (rr + 1) * GRID_W)
                ar = a_ref[sl, :]
                hl = ar * hl + u_ref[sl, :]
                ac = ar * ac
                u_ref[sl, :] = hl
                a_ref[sl, :] = ac
            ccum, hend0 = _scan_rows(ac, hl, rev)
            hend = hend0 + ccum * h0
            crow = lax.broadcasted_iota(jnp.int32, (GRID_W, RG_W), 0)
            if rev:
                hin = jnp.where(crow == GRID_W - 1, h0, pltpu.roll(hend, GRID_W - 1, axis=0))
                last = hend[0:1, :]
            else:
                hin = jnp.where(crow == 0, h0, pltpu.roll(hend, 1, axis=0))
                last = hend[GRID_W - 1:GRID_W, :]
            for rr in range(nrow):
                sl = slice(rr * GRID_W, (rr + 1) * GRID_W)
                u_ref[sl, :] = u_ref[sl, :] + a_ref[sl, :] * hin
            h = u_ref[...]
        if d == 0:
            hsum_ref[...] = h
        else:
            hsum_ref[...] = hsum_ref[...] + h
        if emit_state:
            st_ref[0, d:d + 1, :] = last
    y_ref[...] = (hsum_ref[...] * ggr_ref[...]).astype(BF16)


def _rglru_call(xr, ggr, conv_w, conv_b, wg, bg, lam, h0, seq, grid_rows, emit_state):
    n = xr.shape[0]
    nb = n // seq
    has_init = h0 is not None
    tok_spec = pl.BlockSpec((seq, RG_W), lambda b: (b, 0))
    full = lambda a: pl.BlockSpec(a.shape, lambda b: (0,) * a.ndim)
    in_specs = [tok_spec, tok_spec, full(conv_w), full(conv_b), full(wg), full(bg), full(lam)]
    args = [xr, ggr, conv_w, conv_b, wg, bg, lam]
    if has_init:
        in_specs.append(pl.BlockSpec((1, 2, RG_W), lambda b: (b, 0, 0)))
        args.append(h0)
    out_specs = [tok_spec]
    out_shape = [jax.ShapeDtypeStruct((n, RG_W), BF16)]
    if emit_state:
        out_specs.append(pl.BlockSpec((1, 2, RG_W), lambda b: (b, 0, 0)))
        out_shape.append(jax.ShapeDtypeStruct((nb, 2, RG_W), F32))
    res = pl.pallas_call(
        functools.partial(_rglru_kernel, seq=seq, grid_rows=grid_rows, has_init=has_init,
                          emit_state=emit_state),
        grid=(nb,),
        in_specs=in_specs,
        out_specs=out_specs,
        out_shape=out_shape,
        scratch_shapes=[pltpu.VMEM((seq, RG_W), F32)] * 3,
        compiler_params=_cparams(1, 20 * seq * RG_W * 4 + (8 << 20)),
        name="rglru",
    )(*args)
    return res if emit_state else (res[0], None)


def _oproj_kernel(og_ref, yrg_ref, x_ref, mod_ref, wo_ref, g_ref, b_ref, x1_ref, h2t_ref):
    mod = mod_ref[0]
    g1 = mod[:, 2 * D_MODEL:3 * D_MODEL]
    sh2 = mod[:, 3 * D_MODEL:4 * D_MODEL]
    sc2 = mod[:, 4 * D_MODEL:5 * D_MODEL]
    mix = _dot(og_ref[...], wo_ref[0:HG_W, :]) + _dot(yrg_ref[...], wo_ref[HG_W:, :])
    x1 = _layer_norm(ALPHA * x_ref[...] + g1 * mix, g_ref[...], b_ref[...])
    x1_ref[...] = x1
    h2 = x1 * (1.0 + sc2) + sh2
    h2t_ref[...] = h2.T.astype(BF16)


def _oproj_call(og, yrg, x2d, mod3, w_out_bf, ln_g, ln_b, mod_row_fn):
    n = x2d.shape[0]
    nt = n // TOK_TILE
    return pl.pallas_call(
        _oproj_kernel,
        grid=(nt,),
        in_specs=[
            pl.BlockSpec((TOK_TILE, HG_W), lambda t: (t, 0)),
            pl.BlockSpec((TOK_TILE, RG_W), lambda t: (t, 0)),
            pl.BlockSpec((TOK_TILE, D_MODEL), lambda t: (t, 0)),
            pl.BlockSpec((1, 1, 6 * D_MODEL), lambda t: (mod_row_fn(t), 0, 0)),
            pl.BlockSpec((D_MODEL, D_MODEL), lambda t: (0, 0)),
            pl.BlockSpec((1, D_MODEL), lambda t: (0, 0)),
            pl.BlockSpec((1, D_MODEL), lambda t: (0, 0)),
        ],
        out_specs=[
            pl.BlockSpec((TOK_TILE, D_MODEL), lambda t: (t, 0)),
            pl.BlockSpec((D_MODEL, TOK_TILE), lambda t: (0, t)),
        ],
        out_shape=[
            jax.ShapeDtypeStruct((n, D_MODEL), F32),
            jax.ShapeDtypeStruct((D_MODEL, n), BF16),
        ],
        compiler_params=_cparams(1, 32 << 20),
        name="oproj",
    )(og, yrg, x2d, mod3, w_out_bf, ln_g, ln_b)


def _cmpx(vals, i, j):
    hi = jnp.maximum(vals[i], vals[j])
    lo = jnp.minimum(vals[i], vals[j])
    vals[i] = hi
    vals[j] = lo


def _bitonic_merge_desc(vals):
    n = len(vals)
    d = n // 2
    while d >= 1:
        for i in range(n):
            if (i & d) == 0:
                _cmpx(vals, i, i + d)
        d //= 2
    return vals


def _sort_desc(vals):
    n = len(vals)
    if n == 1:
        return vals
    top = _sort_desc(vals[:n // 2])
    bot = _sort_desc(vals[n // 2:])
    return _bitonic_merge_desc(top + bot[::-1])


def _merge_top(a, b):
    n = len(a)
    return _bitonic_merge_desc([jnp.maximum(a[i], b[n - 1 - i]) for i in range(n)])


def _merge_sublanes(vals):
    s = SUBLANES // 2
    while s >= 1:
        rolled = [pltpu.roll(v, s, axis=0) for v in vals]
        vals = _merge_top(vals, rolled)
        s //= 2
    return vals


def _top_values(s):
    nv = PEER_NKEYS // SUBLANES
    vals = [s[i * SUBLANES:(i + 1) * SUBLANES, :] for i in range(nv)]
    vals = _sort_desc(vals)
    return _merge_sublanes(vals)


def _count_prefix(pred, rows):
    assert len(rows) == 16
    t16 = pred(rows[15])
    t8 = pred(rows[7])
    t4 = pred(jnp.where(t8, rows[11], rows[3]))
    t2 = pred(jnp.where(t8, jnp.where(t4, rows[13], rows[9]), jnp.where(t4, rows[5], rows[1])))
    hi = jnp.where(t4, jnp.where(t2, rows[14], rows[12]), jnp.where(t2, rows[10], rows[8]))
    lo = jnp.where(t4, jnp.where(t2, rows[6], rows[4]), jnp.where(t2, rows[2], rows[0]))
    t1 = pred(jnp.where(t8, hi, lo))
    cnt = (jnp.where(t8, 8.0, 0.0) + jnp.where(t4, 4.0, 0.0)
           + jnp.where(t2, 2.0, 0.0) + jnp.where(t1, 1.0, 0.0))
    return jnp.where(t16, 16.0, cnt)


def _route_kernel(ht_ref, wqt_ref, keys_ref, nb_ref, c0_ref, r1_ref, e1_ref):
    tm = ht_ref.shape[1]
    qt = _dot(wqt_ref[...], ht_ref[...]).astype(BF16)
    half = PEER_DQ // 2
    sub = lax.broadcasted_iota(jnp.int32, (SUBLANES, LANES), 0)
    for h in range(PEER_HEADS):
        sfull = [_dot(keys_ref[2 * h + p], qt[(2 * h + p) * half:(2 * h + p + 1) * half, :])
                 for p in range(2)]
        for lt in range(tm // LANES):
            ls = slice(lt * LANES, (lt + 1) * LANES)
            s = [sfull[0][:, ls], sfull[1][:, ls]]
            top = [_top_values(s[0]), _top_values(s[1])]
            lists = []
            for gb in range(PEER_TOPK // SUBLANES):
                bsel = jnp.zeros((SUBLANES, LANES), F32)
                for b in range(SUBLANES):
                    bsel = jnp.where(sub == b, top[1][gb * SUBLANES + b], bsel)
                lists.append([top[0][a] + bsel for a in range(PEER_TOPK)])
            cand = lists[0]
            for gb in range(1, len(lists)):
                cand = _merge_top(cand, lists[gb])
            fv = _merge_sublanes(cand)
            mx = fv[0][0:1, :]
            zsum = jnp.zeros((1, LANES), F32)
            for kk in range(PEER_TOPK):
                zsum = zsum + jnp.exp(fv[kk][0:1, :] - mx)
            tau = fv[PEER_TOPK - 1][0:1, :]
            sv1 = [top[1][b][0:1, :] for b in range(PEER_TOPK)]
            rb = 4 * SUBLANES
            for r0 in range(0, PEER_NKEYS, rb):
                s0b = s[0][r0:r0 + rb, :]
                s1b = s[1][r0:r0 + rb, :]
                nb_ref[h, r0:r0 + rb, ls] = _count_prefix(lambda r: s0b + r >= tau, sv1)
                r1_ref[h, r0:r0 + rb, ls] = _count_prefix(lambda r: r > s1b, sv1).astype(BF16)
            e1_ref[h, :, ls] = jnp.exp(s[1] - top[1][0][0:1, :]).astype(BF16)
            c0_ref[h, :, ls] = jnp.exp(s[0] - top[0][0][0:1, :]) / zsum


def _route_call(h2t, wqt_bf, keys_bf):
    n = h2t.shape[1]
    nt = n // ROUTE_TM
    big = lambda dt: jax.ShapeDtypeStruct((PEER_HEADS, PEER_NKEYS, n), dt)
    big_spec = pl.BlockSpec((PEER_HEADS, PEER_NKEYS, ROUTE_TM), lambda t: (0, 0, t))
    return pl.pallas_call(
        _route_kernel,
        grid=(nt,),
        in_specs=[
            pl.BlockSpec((D_MODEL, ROUTE_TM), lambda t: (0, t)),
            pl.BlockSpec(wqt_bf.shape, lambda t: (0, 0)),
            pl.BlockSpec(keys_bf.shape, lambda t: (0, 0, 0)),
        ],
        out_specs=[big_spec] * 4,
        out_shape=[big(F32), big(F32), big(BF16), big(BF16)],
        compiler_params=_cparams(1, 40 << 20),
        name="route",
    )(h2t, wqt_bf, keys_bf)


def _peer_kernel(ht_ref, u_ref, vt_ref, nb_ref, c0_ref, r1_ref, e1_ref,
                 x1_ref, mod_ref, g_ref, b_ref, out_ref, pt_ref, wt_ref, acc_ref, re_ref):
    step = pl.program_id(1)
    nsteps = pl.num_programs(1)
    tm = ht_ref.shape[1]
    nloc = PEER_EB // PEER_NKEYS

    @pl.when(step == 0)
    def _():
        acc_ref[...] = jnp.zeros_like(acc_ref)
        re_ref[:, 0, :, 0:tm] = r1_ref[...]
        re_ref[:, 1, :, LANES:LANES + tm] = e1_ref[...]

    pt_ref[...] = _dot(u_ref[...].astype(BF16), ht_ref[...]).astype(BF16)

    pack = 2 * SUBLANES
    npk = PEER_NKEYS // pack
    zero = jnp.zeros((npk, pack, LANES), BF16)
    for mt in range(tm // LANES):
        ls = slice(mt * LANES, (mt + 1) * LANES)
        for iq in range(nloc // PEER_IG):
            gates = [None] * PEER_IG
            for h in range(PEER_HEADS):
                r1 = re_ref[h, 0, :, ls].reshape(npk, pack, LANES)
                e1 = re_ref[h, 1, :, (mt + 1) * LANES:(mt + 2) * LANES].reshape(npk, pack, LANES)
                for k in range(PEER_IG):
                    il = iq * PEER_IG + k
                    nbv = jnp.broadcast_to(nb_ref[h, 0, il:il + 1, ls], (pack, LANES)).astype(BF16)
                    c0v = jnp.broadcast_to(c0_ref[h, 0, il:il + 1, ls], (pack, LANES)).astype(BF16)
                    term = jnp.where(r1 < nbv[None], e1, zero) * c0v[None]
                    gates[k] = term if h == 0 else gates[k] + term
            for k in range(PEER_IG):
                il = iq * PEER_IG + k
                rs = slice(il * PEER_NKEYS, (il + 1) * PEER_NKEYS)
                wt_ref[rs, ls] = gates[k].reshape(PEER_NKEYS, LANES) * _gelu(pt_ref[rs, ls])
    acc_ref[...] += _dot(vt_ref[0], wt_ref[...])

    @pl.when(step == nsteps - 1)
    def _():
        mod = mod_ref[0]
        g2 = mod[:, 5 * D_MODEL:6 * D_MODEL]
        ff = acc_ref[...].T
        out_ref[...] = _layer_norm(ALPHA * x1_ref[...] + g2 * ff, g_ref[...], b_ref[...])


def _peer_call(h2t, u_bf, vt_bf, route, x1, mod3, ln_g, ln_b, mod_row_fn):
    nb, c0, r1, e1 = route
    n = h2t.shape[1]
    nt = n // PEER_TM
    nblk = u_bf.shape[0] // PEER_EB
    big_spec = pl.BlockSpec((PEER_HEADS, PEER_NKEYS, PEER_TM), lambda t, i: (0, 0, t))
    nloc = PEER_EB // PEER_NKEYS
    nb = nb.reshape(PEER_HEADS, PEER_NKEYS // nloc, nloc, n)
    c0 = c0.reshape(PEER_HEADS, PEER_NKEYS // nloc, nloc, n)
    row_spec = pl.BlockSpec((PEER_HEADS, 1, nloc, PEER_TM), lambda t, i: (0, i, 0, t))
    return pl.pallas_call(
        _peer_kernel,
        grid=(nt, nblk),
        in_specs=[
            pl.BlockSpec((D_MODEL, PEER_TM), lambda t, i: (0, t)),
            pl.BlockSpec((PEER_EB, D_MODEL), lambda t, i: (i, 0)),
            pl.BlockSpec((1, D_MODEL, PEER_EB), lambda t, i: (i, 0, 0)),
            row_spec, row_spec, big_spec, big_spec,
            pl.BlockSpec((PEER_TM, D_MODEL), lambda t, i: (t, 0)),
            pl.BlockSpec((1, 1, 6 * D_MODEL), lambda t, i: (mod_row_fn(t), 0, 0)),
            pl.BlockSpec((1, D_MODEL), lambda t, i: (0, 0)),
            pl.BlockSpec((1, D_MODEL), lambda t, i: (0, 0)),
        ],
        out_specs=pl.BlockSpec((PEER_TM, D_MODEL), lambda t, i: (t, 0)),
        out_shape=jax.ShapeDtypeStruct((n, D_MODEL), F32),
        scratch_shapes=[
            pltpu.VMEM((PEER_EB, PEER_TM), BF16),
            pltpu.VMEM((PEER_EB, PEER_TM), BF16),
            pltpu.VMEM((D_MODEL, PEER_TM), F32),
            pltpu.VMEM((PEER_HEADS, 2, PEER_NKEYS, PEER_TM + LANES), BF16),
        ],
        compiler_params=_cparams(2, 56 << 20),
        name="peer",
    )(h2t, u_bf, vt_bf, nb, c0, r1, e1, x1, mod3, ln_g, ln_b)


def _block_diag(w):
    nh, bw, _ = w.shape
    eye = jnp.eye(nh, dtype=w.dtype)
    return (eye[:, None, :, None] * w[:, :, None, :]).reshape(nh * bw, nh * bw)


def _run_path(x, mod3, weights, s_hg0, s_rg0, seq, grid_rows, emit_state, mod_tok_row):
    bsz = x.shape[0]
    x2d = x.reshape(bsz * seq, D_MODEL)
    row256 = lambda t: mod_tok_row(t, TOK_TILE)
    prep = _inproj_call(x2d, mod3, weights["w_in"], weights["lb"], row256)
    qdf, kdf, kef, qdb, kdb, keb, v, vt, dec, sg, xr, ggr = prep
    og, st_hg = _hgrn_call((qdf, kdf, kef, qdb, kdb, keb, v, vt, dec, sg), weights["hg_norm"],
                           s_hg0, seq, emit_state)
    yrg, st_rg = _rglru_call(xr, ggr, weights["conv_w"], weights["conv_b"], weights["wg"],
                             weights["bg"], weights["lam"], s_rg0, seq, grid_rows, emit_state)
    x1, h2t = _oproj_call(og, yrg, x2d, mod3, weights["w_out"], weights["ln1_g"], weights["ln1_b"],
                          row256)
    route = _route_call(h2t, weights["wqt"], weights["keys"])
    out = _peer_call(h2t, weights["u"], weights["vt"], route, x1, mod3, weights["ln2_g"],
                     weights["ln2_b"], lambda t: mod_tok_row(t, PEER_TM))
    return out.reshape(bsz, seq, D_MODEL), st_hg, st_rg


def kernel(x_prompt, x_sample, c, state_hgrn, state_rglru, c_ctx, w_ada, b_ada, w_in, hgrn_lb,
           hgrn_norm_g, conv_w, conv_b, rg_wr, rg_br, rg_wi, rg_bi, rg_lam, w_out, ln1_g, ln1_b,
           peer_wq, peer_keys, peer_u, peer_v, ln2_g, ln2_b):
    assert w_ada.shape[0] == DEPTH
    bp, seq_p, _ = x_prompt.shape
    bs, seq_s, _ = x_sample.shape
    l = 0
    nrows = -(-(1 + bs) // SUBLANES) * SUBLANES
    cond = jnp.zeros((nrows, D_MODEL), F32).at[0].set(c_ctx).at[1:1 + bs].set(c)
    mod = _mod_call(cond, w_ada[l], b_ada[l][None, :])
    mod3 = mod[:, None, :]

    wg = jnp.concatenate([_block_diag(rg_wr[l, 0]), _block_diag(rg_wi[l, 0]),
                          _block_diag(rg_wr[l, 1]), _block_diag(rg_wi[l, 1])], axis=1)
    bg = jnp.concatenate([rg_br[l, 0], rg_bi[l, 0], rg_br[l, 1], rg_bi[l, 1]])[None, :]
    weights = {
        "w_in": w_in[l].astype(BF16),
        "lb": hgrn_lb,
        "hg_norm": hgrn_norm_g[l][None, :],
        "conv_w": conv_w[l],
        "conv_b": conv_b[l][None, :],
        "wg": wg.astype(BF16),
        "bg": bg,
        "lam": rg_lam[l],
        "w_out": w_out[l].astype(BF16),
        "ln1_g": ln1_g[l][None, :],
        "ln1_b": ln1_b[l][None, :],
        "wqt": peer_wq[l].T.astype(BF16),
        "keys": peer_keys[l].reshape(PEER_HEADS * 2, PEER_NKEYS, PEER_DQ // 2).astype(BF16),
        "u": peer_u[l],
        "vt": peer_v[l].reshape(-1, PEER_EB, D_MODEL).transpose(0, 2, 1).astype(BF16),
        "ln2_g": ln2_g[l][None, :],
        "ln2_b": ln2_b[l][None, :],
    }

    yp, st_hg, st_rg = _run_path(x_prompt, mod3, weights, None, None, seq_p, False, True,
                                 lambda t, tile: 0)
    ys, _, _ = _run_path(x_sample, mod3, weights, state_hgrn[:, l], state_rglru[:, l], seq_s, True,
                         False, lambda t, tile: 1 + (t * tile) // seq_s)
    new_hg = st_hg[:, None].astype(x_prompt.dtype)
    new_rg = st_rg[:, None].astype(x_prompt.dtype)
    return (yp, ys, new_hg, new_rg)
```
